```python
import math
import jax
import jax.numpy as jnp
from jax import lax
import numpy as np

D_MODEL = 1024
BATCH = 8
SEQ = 4096
DEPTH = 4

NORM_EPS = 1e-6
D_FF = 2816
FFN_RES_SCALE = 0.5

A_HEADS = 8
A_HEAD_DIM = 64
A_PATTERNS = ((128, 1), (512, 4), (2048, 16))
A_BLOCK = 128

B_HEADS = 4
B_HEAD_DIM = 128
B_CONV = 4
B_CHUNK = 64

C_D_INNER = 2 * D_MODEL
C_HEAD_DIM = 64
C_HEADS = C_D_INNER // C_HEAD_DIM
C_GROUPS = 4
C_STATE = 128
C_CONV = 4
C_CHUNK = 64

A_WIDTH = A_HEADS * A_HEAD_DIM
B_WIDTH = B_HEADS * B_HEAD_DIM
EVEN_IN = 3 * A_WIDTH + 4 * B_WIDTH + 2 * B_HEADS
C_XBC = C_D_INNER + 2 * C_GROUPS * C_STATE
ODD_IN = C_D_INNER + C_XBC + C_HEADS
N_EVEN = (DEPTH + 1) // 2
N_ODD = DEPTH // 2

kernel_name = 'hybrid_dilated_attn_gdn_mamba2_trunk'


def rmsnorm(x, w):
    xf = x.astype(jnp.float32)
    y = xf * lax.rsqrt(jnp.mean(xf * xf, axis=-1, keepdims=True) + NORM_EPS)
    return (y * w.astype(jnp.float32)).astype(x.dtype)


def l2norm(x):
    return x * lax.rsqrt(jnp.sum(x * x, axis=-1, keepdims=True) + NORM_EPS)


def swiglu(x, w_gate, w_up, w_down):
    return (jax.nn.silu(x @ w_gate) * (x @ w_up)) @ w_down


def causal_dwconv(x, w):
    k_len, ch = w.shape
    return lax.conv_general_dilated(x, w[:, None, :].astype(x.dtype), window_strides=(1,),
                                    padding=((k_len - 1, 0),), dimension_numbers=('NWC', 'WIO', 'NWC'),
                                    feature_group_count=ch)


def dilated_window_attention(q, k, v):
    bsz, seq, heads, dh = q.shape
    scale = dh ** -0.5
    nums, maxes, dens = [], [], []
    for window, dil in A_PATTERNS:
        span = window // dil
        length = seq // dil
        blk = math.gcd(length, A_BLOCK)
        nb = length // blk

        def by_residue(t):
            return t.reshape(bsz, length, dil, heads, dh).transpose(0, 2, 1, 3, 4)

        pad = ((0, 0), (0, 0), (span, 0), (0, 0), (0, 0))
        kp = jnp.pad(by_residue(k), pad)
        vp = jnp.pad(by_residue(v), pad)
        idx = jnp.arange(nb)[:, None] * blk + jnp.arange(blk + span)[None, :]
        kb = kp[:, :, idx]
        vb = vp[:, :, idx]
        qb = by_residue(q).reshape(bsz, dil, nb, blk, heads, dh)
        s = jnp.einsum('brnqhd,brnkhd->brnhqk', qb, kb) * scale
        dist = jnp.arange(blk)[:, None] + span - jnp.arange(blk + span)[None, :]
        valid = ((dist >= 0) & (dist <= span))[None] & (idx >= span)[:, None, :]
        s = jnp.where(valid[:, None], s, -jnp.inf)
        m = jnp.max(s, axis=-1, keepdims=True)
        p = jnp.exp(s - m)
        den = jnp.sum(p, axis=-1)
        num = jnp.einsum('brnhqk,brnkhd->brnqhd', p, vb)

        def back(t):
            t = t.reshape(bsz, dil, length, *t.shape[4:])
            return jnp.moveaxis(t, 1, 2).reshape(bsz, seq, *t.shape[3:])

        nums.append(back(num))
        maxes.append(back(jnp.swapaxes(m[..., 0], -1, -2)))
        dens.append(back(jnp.swapaxes(den, -1, -2)))
    mx = jnp.max(jnp.stack(maxes), axis=0)
    wts = [jnp.exp(mi - mx) for mi in maxes]
    num = sum(n * w[..., None] for n, w in zip(nums, wts))
    den = sum(d * w for d, w in zip(dens, wts))
    return num / den[..., None]


def gated_delta_rule(q, k, v, g, beta):
    bsz, seq, heads, dk = q.shape
    dv = v.shape[-1]
    nc = seq // B_CHUNK

    def chunks(t):
        t = t.reshape(bsz, nc, B_CHUNK, heads, *t.shape[3:])
        return jnp.swapaxes(t, 2, 3)

    qc, kc, vc = chunks(q * dk ** -0.5), chunks(k), chunks(v)
    gc = jnp.cumsum(chunks(g), axis=-1)
    bc = chunks(beta)[..., None]
    tri = jnp.tril(jnp.ones((B_CHUNK, B_CHUNK), bool))
    strict = jnp.tril(jnp.ones((B_CHUNK, B_CHUNK), bool), -1)
    decay = jnp.exp(jnp.where(tri, gc[..., :, None] - gc[..., None, :], -jnp.inf))
    k_beta = kc * bc
    a_mat = jnp.where(strict, jnp.einsum('bnhid,bnhjd->bnhij', k_beta, kc) * decay, 0.0)
    t_mat = a_mat + jnp.eye(B_CHUNK, dtype=a_mat.dtype)
    u = lax.linalg.triangular_solve(t_mat, vc * bc, left_side=True, lower=True, unit_diagonal=True)
    w = lax.linalg.triangular_solve(t_mat, k_beta * jnp.exp(gc)[..., None], left_side=True, lower=True,
                                    unit_diagonal=True)
    qk = jnp.where(tri, jnp.einsum('bnhid,bnhjd->bnhij', qc, kc) * decay, 0.0)
    q_dec = qc * jnp.exp(gc)[..., None]
    g_last = gc[..., -1]
    k_dec = kc * jnp.exp(g_last[..., None] - gc)[..., None]

    def step(state, inp):
        u_i, w_i, q_i, qk_i, k_i, gl_i = inp
        v_new = u_i - jnp.einsum('bhck,bhkv->bhcv', w_i, state)
        o_i = jnp.einsum('bhck,bhkv->bhcv', q_i, state) + jnp.einsum('bhij,bhjv->bhiv', qk_i, v_new)
        state = state * jnp.exp(gl_i)[..., None, None] + jnp.einsum('bhck,bhcv->bhkv', k_i, v_new)
        return state, o_i

    xs = tuple(jnp.moveaxis(t, 1, 0) for t in (u, w, q_dec, qk, k_dec, g_last))
    state0 = jnp.zeros((bsz, heads, dk, dv), q.dtype)
    _, o = lax.scan(step, state0, xs)
    return jnp.transpose(o, (1, 0, 3, 2, 4)).reshape(bsz, seq, heads, dv)


def ssd_chunked_scan(x, a, b_in, c_in):
    bsz, seq, heads, p = x.shape
    groups, n = b_in.shape[2:]
    rep = heads // groups
    nc = seq // C_CHUNK
    xc = x.reshape(bsz, nc, C_CHUNK, groups, rep, p)
    acum = jnp.cumsum(a.reshape(bsz, nc, C_CHUNK, groups, rep), axis=2)
    bc = b_in.reshape(bsz, nc, C_CHUNK, groups, n)
    cc = c_in.reshape(bsz, nc, C_CHUNK, groups, n)
    tri = jnp.tril(jnp.ones((C_CHUNK, C_CHUNK), bool))[:, :, None, None]
    seg = jnp.exp(jnp.where(tri, acum[:, :, :, None] - acum[:, :, None, :], -jnp.inf))
    cb = jnp.einsum('bnlgd,bnsgd->bnlsg', cc, bc)
    y_diag = jnp.einsum('bnlsgr,bnsgrp->bnlgrp', cb[..., None] * seg, xc)

    def step(state, inp):
        x_i, a_i, b_i, c_i = inp
        y_off = jnp.einsum('blgd,bgrpd,blgr->blgrp', c_i, state, jnp.exp(a_i))
        a_last = a_i[:, -1]
        state = state * jnp.exp(a_last)[..., None, None] + jnp.einsum(
            'bsgd,bsgr,bsgrp->bgrpd', b_i, jnp.exp(a_last[:, None] - a_i), x_i)
        return state, y_off

    xs = tuple(jnp.moveaxis(t, 1, 0) for t in (xc, acum, bc, cc))
    state0 = jnp.zeros((bsz, groups, rep, p, n), x.dtype)
    _, y_off = lax.scan(step, state0, xs)
    return (y_diag + jnp.moveaxis(y_off, 0, 1)).reshape(bsz, seq, heads, p)


def even_mixer(u, w_in, conv_w, a_log, dt_bias, head_norm_w, w_out):
    bsz, seq, _ = u.shape
    proj = (u @ w_in).astype(jnp.float32)
    cuts = [int(c) for c in np.cumsum([A_WIDTH, A_WIDTH, A_WIDTH, 3 * B_WIDTH, B_WIDTH, B_HEADS])]
    qa, ka, va, qkv_b, z, beta_raw, a_raw = jnp.split(proj, cuts, axis=-1)
    qa, ka, va = (t.reshape(bsz, seq, A_HEADS, A_HEAD_DIM) for t in (qa, ka, va))
    o_a = dilated_window_attention(qa, ka, va)
    qkv_b = jax.nn.silu(causal_dwconv(qkv_b, conv_w))
    qb, kb, vb = (t.reshape(bsz, seq, B_HEADS, B_HEAD_DIM) for t in jnp.split(qkv_b, 3, axis=-1))
    beta = jax.nn.sigmoid(beta_raw)
    g = -jnp.exp(a_log) * jax.nn.softplus(a_raw + dt_bias)
    o_b = gated_delta_rule(l2norm(qb), l2norm(kb), vb, g, beta)
    o_b = rmsnorm(o_b, head_norm_w) * jax.nn.silu(z.reshape(bsz, seq, B_HEADS, B_HEAD_DIM))
    o = jnp.concatenate([o_a.reshape(bsz, seq, A_WIDTH), o_b.reshape(bsz, seq, B_WIDTH)], axis=-1)
    return (o @ w_out).astype(u.dtype)


def odd_mixer(u, w_in, conv_w, conv_b, dt_bias, a_log, d_skip, out_norm_w, w_out):
    bsz, seq, _ = u.shape
    proj = (u @ w_in).astype(jnp.float32)
    z, xbc, dt = jnp.split(proj, [C_D_INNER, C_D_INNER + C_XBC], axis=-1)
    xbc = jax.nn.silu(causal_dwconv(xbc, conv_w) + conv_b)
    xs, b_in, c_in = jnp.split(xbc, [C_D_INNER, C_D_INNER + C_GROUPS * C_STATE], axis=-1)
    xs = xs.reshape(bsz, seq, C_HEADS, C_HEAD_DIM)
    b_in = b_in.reshape(bsz, seq, C_GROUPS, C_STATE)
    c_in = c_in.reshape(bsz, seq, C_GROUPS, C_STATE)
    dt = jax.nn.softplus(dt + dt_bias)
    a = -jnp.exp(a_log) * dt
    y = ssd_chunked_scan(xs * dt[..., None], a, b_in, c_in) + d_skip[:, None] * xs
    y = y.reshape(bsz, seq, C_D_INNER) * jax.nn.silu(z)
    y = rmsnorm(y.reshape(bsz, seq, C_GROUPS, C_D_INNER // C_GROUPS), out_norm_w.reshape(C_GROUPS, -1))
    return (y.reshape(bsz, seq, C_D_INNER) @ w_out).astype(u.dtype)


def _dt_bias(key, shape):
    dt = jnp.exp(jax.random.uniform(key, shape, jnp.float32, math.log(1e-3), math.log(1e-1)))
    return dt + jnp.log(-jnp.expm1(-dt))


def setup_inputs(seed: int = 0) -> dict:
    key = jax.random.key(seed)
    ks = iter(jax.random.split(key, 24))

    def nrm(shape, scale):
        return scale * jax.random.normal(next(ks), shape, jnp.float32)

    x = nrm((BATCH, SEQ, D_MODEL), 1.0)
    norm_w = 1.0 + nrm((DEPTH, 6, D_MODEL), 0.05)
    ffn_w_gate = nrm((DEPTH, 2, D_MODEL, D_FF), D_MODEL ** -0.5)
    ffn_w_up = nrm((DEPTH, 2, D_MODEL, D_FF), D_MODEL ** -0.5)
    ffn_w_down = nrm((DEPTH, 2, D_FF, D_MODEL), D_FF ** -0.5)
    even_w_in = nrm((N_EVEN, D_MODEL, EVEN_IN), D_MODEL ** -0.5)
    even_conv_w = nrm((N_EVEN, B_CONV, 3 * B_WIDTH), B_CONV ** -0.5)
    even_a_log = jnp.log(jax.random.uniform(next(ks), (N_EVEN, B_HEADS), jnp.float32, 1.0, 16.0))
    even_dt_bias = _dt_bias(next(ks), (N_EVEN, B_HEADS))
    even_head_norm_w = 1.0 + nrm((N_EVEN, B_HEAD_DIM), 0.05)
    even_w_out = nrm((N_EVEN, A_WIDTH + B_WIDTH, D_MODEL), (A_WIDTH + B_WIDTH) ** -0.5)
    odd_w_in = nrm((N_ODD, D_MODEL, ODD_IN), D_MODEL ** -0.5)
    odd_conv_w = nrm((N_ODD, C_CONV, C_XBC), C_CONV ** -0.5)
    odd_conv_b = nrm((N_ODD, C_XBC), 0.02)
    odd_dt_bias = _dt_bias(next(ks), (N_ODD, C_HEADS))
    odd_a_log = jnp.log(jax.random.uniform(next(ks), (N_ODD, C_HEADS), jnp.float32, 1.0, 16.0))
    odd_d_skip = 1.0 + nrm((N_ODD, C_HEADS), 0.05)
    odd_out_norm_w = 1.0 + nrm((N_ODD, C_D_INNER), 0.05)
    odd_w_out = nrm((N_ODD, C_D_INNER, D_MODEL), C_D_INNER ** -0.5)
    return {'x': x, 'norm_w': norm_w, 'ffn_w_gate': ffn_w_gate, 'ffn_w_up': ffn_w_up,
            'ffn_w_down': ffn_w_down, 'even_w_in': even_w_in, 'even_conv_w': even_conv_w,
            'even_a_log': even_a_log, 'even_dt_bias': even_dt_bias, 'even_head_norm_w': even_head_norm_w,
            'even_w_out': even_w_out, 'odd_w_in': odd_w_in, 'odd_conv_w': odd_conv_w,
            'odd_conv_b': odd_conv_b, 'odd_dt_bias': odd_dt_bias, 'odd_a_log': odd_a_log,
            'odd_d_skip': odd_d_skip, 'odd_out_norm_w': odd_out_norm_w, 'odd_w_out': odd_w_out}


def reference(x, norm_w, ffn_w_gate, ffn_w_up, ffn_w_down, even_w_in, even_conv_w, even_a_log,
              even_dt_bias, even_head_norm_w, even_w_out, odd_w_in, odd_conv_w, odd_conv_b, odd_dt_bias,
              odd_a_log, odd_d_skip, odd_out_norm_w, odd_w_out):
    h = x
    for layer in range(DEPTH):
        nw = norm_w[layer]
        i = layer // 2
        f = swiglu(rmsnorm(h, nw[0]), ffn_w_gate[layer, 0], ffn_w_up[layer, 0], ffn_w_down[layer, 0])
        h = h + FFN_RES_SCALE * rmsnorm(f, nw[1])
        u = rmsnorm(h, nw[2])
        if layer % 2 == 0:
            m = even_mixer(u, even_w_in[i], even_conv_w[i], even_a_log[i], even_dt_bias[i],
                           even_head_norm_w[i], even_w_out[i])
        else:
            m = odd_mixer(u, odd_w_in[i], odd_conv_w[i], odd_conv_b[i], odd_dt_bias[i], odd_a_log[i],
                          odd_d_skip[i], odd_out_norm_w[i], odd_w_out[i])
        h = h + rmsnorm(m, nw[3])
        f = swiglu(rmsnorm(h, nw[4]), ffn_w_gate[layer, 1], ffn_w_up[layer, 1], ffn_w_down[layer, 1])
        h = h + FFN_RES_SCALE * rmsnorm(f, nw[5])
    return h
```

```python
import functools
import math

import jax
import jax.numpy as jnp
from jax import lax
from jax.experimental import pallas as pl
from jax.experimental.pallas import tpu as pltpu

F32 = jnp.float32
BF16 = jnp.bfloat16
HIGHEST = lax.Precision.HIGHEST

NORM_EPS = 1e-6
FFN_RES_SCALE = 0.5

A_HEADS = 8
A_HEAD_DIM = 64
A_WIDTH = A_HEADS * A_HEAD_DIM
A_DILATIONS = (1, 4, 16)
A_BLOCK = 128

B_HEADS = 4
B_HEAD_DIM = 128
B_WIDTH = B_HEADS * B_HEAD_DIM
B_CONV = 4
B_CHUNK = 64

C_D_INNER = 2048
C_HEAD_DIM = 64
C_HEADS = C_D_INNER // C_HEAD_DIM
C_GROUPS = 4
C_STATE = 128
C_CONV = 4
C_XBC = C_D_INNER + 2 * C_GROUPS * C_STATE
C_GROUP_WIDTH = C_D_INNER // C_GROUPS
C_HEADS_PER_GROUP = C_HEADS // C_GROUPS

LANES = 128
CONV_TAIL = 8
NEG_BIG = -1e30

ROW_TILE = 512
GDN_CHUNKS_PER_STEP = 2
SSD_CHUNK = 128
VMEM_LIMIT = 56 * 1024 * 1024


def _cparams(*sem):
    return pltpu.CompilerParams(dimension_semantics=sem, vmem_limit_bytes=VMEM_LIMIT)


def _dot(a, b, precision=None):
    return jnp.dot(a, b, preferred_element_type=F32, precision=precision)


def _dot_nt(a, b, precision=None):
    return lax.dot_general(a, b, (((1,), (1,)), ((), ())), preferred_element_type=F32, precision=precision)


def _dot_tn(a, b, precision=None):
    return lax.dot_general(a, b, (((0,), (0,)), ((), ())), preferred_element_type=F32, precision=precision)


def _rms(x, w):
    return x * lax.rsqrt(jnp.mean(x * x, axis=-1, keepdims=True) + NORM_EPS) * w


def _silu(x):
    return x * jax.nn.sigmoid(x)


def _softplus(x):
    return jnp.maximum(x, 0.0) + jnp.log1p(jnp.exp(-jnp.abs(x)))


def _const_spec(shape):
    return pl.BlockSpec(shape, lambda *_: (0,) * len(shape), pipeline_mode=pl.Buffered(1))


def _ffn_kernel(h_ref, pre_ref, post_ref, wg_ref, wu_ref, wd_ref, o_ref):
    x = h_ref[...]
    xn = _rms(x, pre_ref[...]).astype(BF16)
    gate = _dot(xn, wg_ref[...])
    up = _dot(xn, wu_ref[...])
    act = (_silu(gate) * up).astype(BF16)
    f = _dot(act, wd_ref[...])
    o_ref[...] = x + FFN_RES_SCALE * _rms(f, post_ref[...])


def _ffn(h, pre_w, post_w, wg, wu, wd):
    t, d = h.shape
    ff = wg.shape[1]
    tm = min(ROW_TILE, t)
    row = pl.BlockSpec((tm, d), lambda i: (i, 0))
    return pl.pallas_call(
        _ffn_kernel,
        grid=(t // tm,),
        in_specs=[row, _const_spec((1, d)), _const_spec((1, d)), _const_spec((d, ff)), _const_spec((d, ff)),
                  _const_spec((ff, d))],
        out_specs=row,
        out_shape=jax.ShapeDtypeStruct((t, d), F32),
        compiler_params=_cparams("parallel"),
        name="ffn",
    )(h, pre_w, post_w, wg, wu, wd)


def _norm_proj_kernel(h_ref, nw_ref, *refs):
    n_out = len(refs) // 2
    xn = _rms(h_ref[...], nw_ref[...]).astype(BF16)
    for w_ref, o_ref in zip(refs[:n_out], refs[n_out:]):
        o_ref[...] = _dot(xn, w_ref[...]).astype(o_ref.dtype)


def _norm_proj(h, nw, weights, out_dtypes):
    t, d = h.shape
    tm = min(ROW_TILE, t)
    row = lambda n: pl.BlockSpec((tm, n), lambda i: (i, 0))
    return pl.pallas_call(
        _norm_proj_kernel,
        grid=(t // tm,),
        in_specs=[row(d), _const_spec((1, d))] + [_const_spec(w.shape) for w in weights],
        out_specs=[row(w.shape[1]) for w in weights],
        out_shape=[jax.ShapeDtypeStruct((t, w.shape[1]), dt) for w, dt in zip(weights, out_dtypes)],
        compiler_params=_cparams("parallel"),
        name="norm_proj",
    )(h, nw, *weights)


def _proj_res_kernel(h_ref, nw_ref, *refs):
    n_in = (len(refs) - 1) // 2
    o_ref = refs[-1]
    m = None
    for x_ref, w_ref in zip(refs[:n_in], refs[n_in:2 * n_in]):
        part = _dot(x_ref[...], w_ref[...])
        m = part if m is None else m + part
    o_ref[...] = h_ref[...] + _rms(m, nw_ref[...])


def _proj_res(h, nw, xs, weights):
    t, d = h.shape
    tm = min(ROW_TILE, t)
    row = lambda n: pl.BlockSpec((tm, n), lambda i: (i, 0))
    return pl.pallas_call(
        _proj_res_kernel,
        grid=(t // tm,),
        in_specs=[row(d), _const_spec((1, d))] + [row(x.shape[1]) for x in xs] + [_const_spec(w.shape) for w in weights],
        out_specs=row(d),
        out_shape=jax.ShapeDtypeStruct((t, d), F32),
        compiler_params=_cparams("parallel"),
        name="proj_res",
    )(h, nw, *xs, *weights)


def _attn_kernel(*refs, first, last):
    q_ref, kp_ref, kc_ref, vp_ref, vc_ref = refs[:5]
    if first:
        acc_o, st_o = refs[5:]
    elif last:
        acc_i, st_i, o_ref = refs[5:]
    else:
        acc_i, st_i, acc_o, st_o = refs[5:]
    blk = q_ref.shape[0]
    n = pl.program_id(2)
    ri = lax.broadcasted_iota(jnp.int32, (blk, blk), 0)
    ci = lax.broadcasted_iota(jnp.int32, (blk, blk), 1)
    prev_valid = ci >= ri
    cur_valid = ci <= ri
    no_prev = jnp.where(n > 0, 0.0, NEG_BIG)
    if not last:
        lane = lax.broadcasted_iota(jnp.int32, (blk, LANES), 1)
        stats = jnp.zeros((blk, LANES), F32)
    for hd in range(A_HEADS):
        sl = slice(hd * A_HEAD_DIM, (hd + 1) * A_HEAD_DIM)
        qh = q_ref[:, sl]
        sp = jnp.where(prev_valid, _dot_nt(qh, kp_ref[:, sl]), NEG_BIG) + no_prev
        sc = jnp.where(cur_valid, _dot_nt(qh, kc_ref[:, sl]), NEG_BIG)
        m = jnp.maximum(jnp.max(sp, axis=-1, keepdims=True), jnp.max(sc, axis=-1, keepdims=True))
        pp = jnp.exp(sp - m)
        pc = jnp.exp(sc - m)
        den = jnp.sum(pp, axis=-1, keepdims=True) + jnp.sum(pc, axis=-1, keepdims=True)
        num = _dot(pp.astype(BF16), vp_ref[:, sl]) + _dot(pc.astype(BF16), vc_ref[:, sl])
        if not first:
            m_in = st_i[:, hd:hd + 1]
            l_in = st_i[:, A_HEADS + hd:A_HEADS + hd + 1]
            m_new = jnp.maximum(m_in, m)
            w_in = jnp.exp(m_in - m_new)
            w_cur = jnp.exp(m - m_new)
            num = acc_i[:, sl] * w_in + num * w_cur
            den = l_in * w_in + den * w_cur
            m = m_new
        if last:
            o_ref[:, sl] = (num / den).astype(o_ref.dtype)
        else:
            acc_o[:, sl] = num
            stats = jnp.where(lane == hd, m, stats)
            stats = jnp.where(lane == A_HEADS + hd, den, stats)
    if not last:
        st_o[...] = stats


def _attn_pattern(q, k, v, carry, dil, first, last, bsz, seq):
    length = seq // dil
    nb = length // A_BLOCK
    view = lambda t, width: t.reshape(bsz, length, dil * width)
    cur = lambda width: pl.BlockSpec((None, A_BLOCK, width), lambda b, r, n: (b, n, r))
    prev = lambda width: pl.BlockSpec((None, A_BLOCK, width), lambda b, r, n: (b, jnp.maximum(n - 1, 0), r))
    qv, kv, vv = view(q, A_WIDTH), view(k, A_WIDTH), view(v, A_WIDTH)
    operands = [qv, kv, kv, vv, vv]
    in_specs = [cur(A_WIDTH), prev(A_WIDTH), cur(A_WIDTH), prev(A_WIDTH), cur(A_WIDTH)]
    if not first:
        acc, stats = carry
        operands += [view(acc, A_WIDTH), view(stats, LANES)]
        in_specs += [cur(A_WIDTH), cur(LANES)]
    t = bsz * seq
    if last:
        out_specs = cur(A_WIDTH)
        out_shape = jax.ShapeDtypeStruct((bsz, length, dil * A_WIDTH), BF16)
    else:
        out_specs = [cur(A_WIDTH), cur(LANES)]
        out_shape = [jax.ShapeDtypeStruct((bsz, length, dil * A_WIDTH), F32),
                     jax.ShapeDtypeStruct((bsz, length, dil * LANES), F32)]
    out = pl.pallas_call(
        functools.partial(_attn_kernel, first=first, last=last),
        grid=(bsz, dil, nb),
        in_specs=in_specs,
        out_specs=out_specs,
        out_shape=out_shape,
        compiler_params=_cparams("parallel", "parallel", "arbitrary"),
        name=f"attn_d{dil}",
    )(*operands)
    if last:
        return out.reshape(t, A_WIDTH)
    return out[0].reshape(t, A_WIDTH), out[1].reshape(t, LANES)


def _dilated_attention(q, k, v, bsz, seq):
    carry = None
    for i, dil in enumerate(A_DILATIONS):
        carry = _attn_pattern(q, k, v, carry, dil, i == 0, i == len(A_DILATIONS) - 1, bsz, seq)
    return carry


def _causal_conv(x_ref, w_ref, xbuf, first_step):
    rows = x_ref.shape[0]
    k_len = w_ref.shape[0]

    @pl.when(first_step)
    def _():
        xbuf[0:CONV_TAIL, :] = jnp.zeros((CONV_TAIL, xbuf.shape[1]), F32)

    xbuf[CONV_TAIL:CONV_TAIL + rows, :] = x_ref[...]
    y = None
    for k in range(k_len):
        term = w_ref[k:k + 1, :] * xbuf[pl.ds(CONV_TAIL - k_len + 1 + k, rows), :]
        y = term if y is None else y + term
    xbuf[0:CONV_TAIL, :] = xbuf[rows:rows + CONV_TAIL, :]
    return y


def _unit_lower_inverse(a_strict, eye):
    p = -a_strict
    x = eye + p
    for _ in range(int(math.log2(a_strict.shape[0])) - 1):
        p = _dot(p, p, HIGHEST)
        x = x + _dot(x, p, HIGHEST)
    return x


def _gdn_kernel(qkv_ref, z_ref, sm_ref, cw_ref, alog_ref, dtb_ref, hnw_ref, o_ref, xbuf, state):
    rows = qkv_ref.shape[0]
    first_step = pl.program_id(1) == 0
    y = _silu(_causal_conv(qkv_ref, cw_ref, xbuf, first_step))

    @pl.when(first_step)
    def _():
        state[...] = jnp.zeros(state.shape, F32)

    sm = sm_ref[...]
    beta = jax.nn.sigmoid(sm)
    g = -jnp.exp(alog_ref[...]) * _softplus(sm + dtb_ref[...])
    ri = lax.broadcasted_iota(jnp.int32, (rows, rows), 0)
    ci = lax.broadcasted_iota(jnp.int32, (rows, rows), 1)
    same_chunk = (ri // B_CHUNK) == (ci // B_CHUNK)
    gc = _dot(jnp.where(same_chunk & (ci <= ri), 1.0, 0.0), g, HIGHEST)
    gc_t = gc.T
    cri = lax.broadcasted_iota(jnp.int32, (B_CHUNK, B_CHUNK), 0)
    cci = lax.broadcasted_iota(jnp.int32, (B_CHUNK, B_CHUNK), 1)
    tri = cci <= cri
    strict = cci < cri
    eye = jnp.where(cci == cri, 1.0, 0.0)
    hnw = hnw_ref[...]
    for c in range(rows // B_CHUNK):
        r0 = c * B_CHUNK
        rs = slice(r0, r0 + B_CHUNK)
        for hd in range(B_HEADS):
            col = lambda base: slice(base + hd * B_HEAD_DIM, base + (hd + 1) * B_HEAD_DIM)
            qh, kh, vh = y[rs, col(0)], y[rs, col(B_WIDTH)], y[rs, col(2 * B_WIDTH)]
            qh = qh * (lax.rsqrt(jnp.sum(qh * qh, axis=-1, keepdims=True) + NORM_EPS) * B_HEAD_DIM ** -0.5)
            kh = kh * lax.rsqrt(jnp.sum(kh * kh, axis=-1, keepdims=True) + NORM_EPS)
            b_col = beta[rs, hd:hd + 1]
            g_col = gc[rs, B_HEADS + hd:B_HEADS + hd + 1]
            g_row = gc_t[B_HEADS + hd:B_HEADS + hd + 1, rs]
            g_last = gc[r0 + B_CHUNK - 1:r0 + B_CHUNK, B_HEADS + hd:B_HEADS + hd + 1]
            decay = jnp.exp(jnp.where(tri, g_col - g_row, NEG_BIG))
            kb = kh.astype(BF16)
            a_mat = jnp.where(strict, _dot_nt(kb, kb) * decay, 0.0) * b_col
            qk = jnp.where(tri, _dot_nt(qh.astype(BF16), kb) * decay, 0.0)
            t_inv = _unit_lower_inverse(a_mat, eye)
            e_col = jnp.exp(g_col)
            u = _dot(t_inv, vh * b_col, HIGHEST)
            w = _dot(t_inv, kh * (b_col * e_col), HIGHEST)
            q_dec = (qh * e_col).astype(BF16)
            k_dec = (kh * jnp.exp(g_last - g_col)).astype(BF16)
            s = state[hd]
            sb = s.astype(BF16)
            v_new = u - _dot(w.astype(BF16), sb)
            vb = v_new.astype(BF16)
            o = _dot(q_dec, sb) + _dot(qk.astype(BF16), vb)
            state[hd] = s * jnp.exp(g_last) + _dot_tn(k_dec, vb)
            o = _rms(o, hnw) * _silu(z_ref[rs, col(0)])
            o_ref[rs, col(0)] = o.astype(o_ref.dtype)


def _gdn(qkv, z, sm, conv_w, alog_row, dtb_row, hnw, bsz, seq):
    rows = GDN_CHUNKS_PER_STEP * B_CHUNK
    steps = seq // rows
    width = qkv.shape[1]
    row = lambda n: pl.BlockSpec((rows, n), lambda b, s: (b * steps + s, 0))
    return pl.pallas_call(
        _gdn_kernel,
        grid=(bsz, steps),
        in_specs=[row(width), row(B_WIDTH), row(LANES), _const_spec(conv_w.shape), _const_spec((1, LANES)),
                  _const_spec((1, LANES)), _const_spec((1, B_HEAD_DIM))],
        out_specs=row(B_WIDTH),
        out_shape=jax.ShapeDtypeStruct((bsz * seq, B_WIDTH), BF16),
        scratch_shapes=[pltpu.VMEM((rows + CONV_TAIL, width), F32),
                        pltpu.VMEM((B_HEADS, B_HEAD_DIM, B_HEAD_DIM), F32)],
        compiler_params=_cparams("parallel", "arbitrary"),
        name="gdn",
    )(qkv, z, sm, conv_w, alog_row, dtb_row, hnw)


def _ssd_kernel(xbc_ref, z_ref, dt_ref, cw_ref, cb_ref, dtb_ref, alog_ref, dsk_ref, onw_ref, o_ref, xbuf, state):
    rows = xbc_ref.shape[0]
    first_step = pl.program_id(1) == 0
    y = _silu(_causal_conv(xbc_ref, cw_ref, xbuf, first_step) + cb_ref[...])

    @pl.when(first_step)
    def _():
        state[...] = jnp.zeros(state.shape, F32)

    dt = _softplus(dt_ref[...] + dtb_ref[...])
    a = -jnp.exp(alog_ref[...]) * dt
    ri = lax.broadcasted_iota(jnp.int32, (rows, rows), 0)
    ci = lax.broadcasted_iota(jnp.int32, (rows, rows), 1)
    tri = ci <= ri
    acum = _dot(jnp.where(tri, 1.0, 0.0), a, HIGHEST)
    acum_t = acum.T
    a_last = acum[rows - 1:rows, :]
    e_acum = jnp.exp(acum)
    e_rem = jnp.exp(a_last - acum)
    e_last = jnp.exp(a_last)
    dsk = dsk_ref[...]
    onw = onw_ref[...]
    for g in range(C_GROUPS):
        b_g = y[:, C_D_INNER + g * C_STATE:C_D_INNER + (g + 1) * C_STATE].astype(BF16)
        c_g = y[:, C_D_INNER + (C_GROUPS + g) * C_STATE:C_D_INNER + (C_GROUPS + g + 1) * C_STATE].astype(BF16)
        cb = _dot_nt(c_g, b_g)
        st = state[g]
        y_off = _dot(c_g, st.astype(BF16))
        outs, xdecs, keeps = [], [], []
        for r in range(C_HEADS_PER_GROUP):
            hd = g * C_HEADS_PER_GROUP + r
            sl = slice(hd * C_HEAD_DIM, (hd + 1) * C_HEAD_DIM)
            lsl = slice(r * C_HEAD_DIM, (r + 1) * C_HEAD_DIM)
            seg = jnp.exp(jnp.where(tri, acum[:, hd:hd + 1] - acum_t[hd:hd + 1, :], NEG_BIG))
            x_h = y[:, sl]
            xdt = x_h * dt[:, hd:hd + 1]
            y_h = _dot((cb * seg).astype(BF16), xdt.astype(BF16))
            y_h = y_h + y_off[:, lsl] * e_acum[:, hd:hd + 1] + dsk[:, sl] * x_h
            outs.append(y_h * _silu(z_ref[:, sl]))
            xdecs.append((xdt * e_rem[:, hd:hd + 1]).astype(BF16))
            keeps.append(jnp.broadcast_to(e_last[:, hd:hd + 1], (1, C_HEAD_DIM)))
        state[g] = st * jnp.concatenate(keeps, axis=1) + _dot_tn(b_g, jnp.concatenate(xdecs, axis=1))
        gsl = slice(g * C_GROUP_WIDTH, (g + 1) * C_GROUP_WIDTH)
        o_ref[:, gsl] = _rms(jnp.concatenate(outs, axis=1), onw[:, gsl]).astype(o_ref.dtype)


def _ssd(xbc, z, dt, conv_w, conv_b, dtb_row, alog_row, dsk_row, onw_row, bsz, seq):
    rows = SSD_CHUNK
    steps = seq // rows
    row = lambda n: pl.BlockSpec((rows, n), lambda b, s: (b * steps + s, 0))
    return pl.pallas_call(
        _ssd_kernel,
        grid=(bsz, steps),
        in_specs=[row(C_XBC), row(C_D_INNER), row(LANES), _const_spec(conv_w.shape), _const_spec((1, C_XBC)),
                  _const_spec((1, LANES)), _const_spec((1, LANES)), _const_spec((1, C_D_INNER)),
                  _const_spec((1, C_D_INNER))],
        out_specs=row(C_D_INNER),
        out_shape=jax.ShapeDtypeStruct((bsz * seq, C_D_INNER), BF16),
        scratch_shapes=[pltpu.VMEM((rows + CONV_TAIL, C_XBC), F32),
                        pltpu.VMEM((C_GROUPS, C_STATE, C_GROUP_WIDTH), F32)],
        compiler_params=_cparams("parallel", "arbitrary"),
        name="ssd",
    )(xbc, z, dt, conv_w, conv_b, dtb_row, alog_row, dsk_row, onw_row)


def _pad_lanes(x, offset=0):
    x = x.reshape(-1, x.shape[-1])
    return jnp.pad(x, ((0, 0), (offset, LANES - offset - x.shape[-1])))


def _even_mixer(h, pre_w, post_w, w_in, conv_w, a_log, dt_bias, head_norm_w, w_out, bsz, seq):
    cuts = [A_WIDTH, 2 * A_WIDTH, 3 * A_WIDTH, 3 * A_WIDTH + 3 * B_WIDTH, 3 * A_WIDTH + 4 * B_WIDTH]
    w_q = (w_in[:, :cuts[0]] * A_HEAD_DIM ** -0.5).astype(BF16)
    w_k = w_in[:, cuts[0]:cuts[1]].astype(BF16)
    w_v = w_in[:, cuts[1]:cuts[2]].astype(BF16)
    w_qkv = w_in[:, cuts[2]:cuts[3]].astype(BF16)
    w_z = w_in[:, cuts[3]:cuts[4]].astype(BF16)
    w_sm = jnp.pad(w_in[:, cuts[4]:], ((0, 0), (0, LANES - 2 * B_HEADS))).astype(BF16)
    qa, ka, va, qkv_b, z, sm = _norm_proj(h, pre_w, [w_q, w_k, w_v, w_qkv, w_z, w_sm],
                                          [BF16, BF16, BF16, F32, F32, F32])
    o_a = _dilated_attention(qa, ka, va, bsz, seq)
    o_b = _gdn(qkv_b, z, sm, conv_w, _pad_lanes(a_log, B_HEADS), _pad_lanes(dt_bias, B_HEADS),
               head_norm_w.reshape(1, -1), bsz, seq)
    w_o = w_out.astype(BF16)
    return _proj_res(h, post_w, [o_a, o_b], [w_o[:A_WIDTH], w_o[A_WIDTH:]])


def _odd_mixer(h, pre_w, post_w, w_in, conv_w, conv_b, dt_bias, a_log, d_skip, out_norm_w, w_out, bsz, seq):
    w_z = w_in[:, :C_D_INNER].astype(BF16)
    w_xbc = w_in[:, C_D_INNER:C_D_INNER + C_XBC].astype(BF16)
    w_dt = jnp.pad(w_in[:, C_D_INNER + C_XBC:], ((0, 0), (0, LANES - C_HEADS))).astype(BF16)
    z, xbc, dt = _norm_proj(h, pre_w, [w_z, w_xbc, w_dt], [F32, F32, F32])
    y = _ssd(xbc, z, dt, conv_w, conv_b.reshape(1, -1), _pad_lanes(dt_bias), _pad_lanes(a_log),
             jnp.repeat(d_skip, C_HEAD_DIM).reshape(1, -1), out_norm_w.reshape(1, -1), bsz, seq)
    return _proj_res(h, post_w, [y], [w_out.astype(BF16)])


def kernel(x, norm_w, ffn_w_gate, ffn_w_up, ffn_w_down, even_w_in, even_conv_w, even_a_log, even_dt_bias,
           even_head_norm_w, even_w_out, odd_w_in, odd_conv_w, odd_conv_b, odd_dt_bias, odd_a_log, odd_d_skip,
           odd_out_norm_w, odd_w_out):
    bsz, seq, d = x.shape
    depth = norm_w.shape[0]
    h = x.reshape(bsz * seq, d)
    wg, wu, wd = ffn_w_gate.astype(BF16), ffn_w_up.astype(BF16), ffn_w_down.astype(BF16)
    for layer in range(depth):
        nw = norm_w[layer][:, None, :]
        i = layer // 2
        h = _ffn(h, nw[0], nw[1], wg[layer, 0], wu[layer, 0], wd[layer, 0])
        if layer % 2 == 0:
            h = _even_mixer(h, nw[2], nw[3], even_w_in[i], even_conv_w[i], even_a_log[i], even_dt_bias[i],
                            even_head_norm_w[i], even_w_out[i], bsz, seq)
        else:
            h = _odd_mixer(h, nw[2], nw[3], odd_w_in[i], odd_conv_w[i], odd_conv_b[i], odd_dt_bias[i],
                           odd_a_log[i], odd_d_skip[i], odd_out_norm_w[i], odd_w_out[i], bsz, seq)
        h = _ffn(h, nw[4], nw[5], wg[layer, 1], wu[layer, 1], wd[layer, 1])
    return h.reshape(bsz, seq, d)
```

```python
import functools
import math

import jax
import jax.numpy as jnp
from jax import lax
from jax.experimental import pallas as pl
from jax.experimental.pallas import tpu as pltpu

F32 = jnp.float32
BF16 = jnp.bfloat16
HIGHEST = lax.Precision.HIGHEST

NORM_EPS = 1e-6
FFN_RES_SCALE = 0.5

A_HEADS = 8
A_HEAD_DIM = 64
A_WIDTH = A_HEADS * A_HEAD_DIM
A_DILATIONS = (1, 4, 16)
A_BLOCK = 128

B_HEADS = 4
B_HEAD_DIM = 128
B_WIDTH = B_HEADS * B_HEAD_DIM
B_CONV = 4

C_D_INNER = 2048
C_HEAD_DIM = 64
C_HEADS = C_D_INNER // C_HEAD_DIM
C_GROUPS = 4
C_STATE = 128
C_CONV = 4
C_XBC = C_D_INNER + 2 * C_GROUPS * C_STATE
C_GROUP_WIDTH = C_D_INNER // C_GROUPS
C_HEADS_PER_GROUP = C_HEADS // C_GROUPS

LANES = 128
CONV_TAIL = 8
NEG_BIG = -1e30

ROW_TILE = 512
GDN_CHUNK = 128
GDN_CHUNKS_PER_STEP = 2
SSD_CHUNK = 128
VMEM_LIMIT = 56 * 1024 * 1024


def _cparams(*sem):
    return pltpu.CompilerParams(dimension_semantics=sem, vmem_limit_bytes=VMEM_LIMIT)


def _dot(a, b, precision=None):
    return jnp.dot(a, b, preferred_element_type=F32, precision=precision)


def _dot_nt(a, b, precision=None):
    return lax.dot_general(a, b, (((1,), (1,)), ((), ())), preferred_element_type=F32, precision=precision)


def _dot_tn(a, b, precision=None):
    return lax.dot_general(a, b, (((0,), (0,)), ((), ())), preferred_element_type=F32, precision=precision)


def _rms(x, w):
    return x * lax.rsqrt(jnp.mean(x * x, axis=-1, keepdims=True) + NORM_EPS) * w


def _silu(x):
    return x * jax.nn.sigmoid(x)


def _softplus(x):
    return jnp.maximum(x, 0.0) + jnp.log1p(jnp.exp(-jnp.abs(x)))


def _const_spec(shape):
    return pl.BlockSpec(shape, lambda *_: (0,) * len(shape), pipeline_mode=pl.Buffered(1))


def _ffn_kernel(h_ref, pre_ref, post_ref, wg_ref, wu_ref, wd_ref, o_ref):
    x = h_ref[...]
    xn = _rms(x, pre_ref[...]).astype(BF16)
    gate = _dot(xn, wg_ref[...])
    up = _dot(xn, wu_ref[...])
    act = (_silu(gate) * up).astype(BF16)
    f = _dot(act, wd_ref[...])
    o_ref[...] = x + FFN_RES_SCALE * _rms(f, post_ref[...])


def _ffn(h, pre_w, post_w, wg, wu, wd):
    t, d = h.shape
    ff = wg.shape[1]
    tm = min(ROW_TILE, t)
    row = pl.BlockSpec((tm, d), lambda i: (i, 0))
    return pl.pallas_call(
        _ffn_kernel,
        grid=(t // tm,),
        in_specs=[row, _const_spec((1, d)), _const_spec((1, d)), _const_spec((d, ff)), _const_spec((d, ff)),
                  _const_spec((ff, d))],
        out_specs=row,
        out_shape=jax.ShapeDtypeStruct((t, d), F32),
        compiler_params=_cparams("parallel"),
        name="ffn",
    )(h, pre_w, post_w, wg, wu, wd)


def _norm_proj_kernel(h_ref, nw_ref, *refs):
    n_out = len(refs) // 2
    xn = _rms(h_ref[...], nw_ref[...]).astype(BF16)
    for w_ref, o_ref in zip(refs[:n_out], refs[n_out:]):
        o_ref[...] = _dot(xn, w_ref[...]).astype(o_ref.dtype)


def _norm_proj(h, nw, weights, out_dtypes):
    t, d = h.shape
    tm = min(ROW_TILE, t)
    row = lambda n: pl.BlockSpec((tm, n), lambda i: (i, 0))
    return pl.pallas_call(
        _norm_proj_kernel,
        grid=(t // tm,),
        in_specs=[row(d), _const_spec((1, d))] + [_const_spec(w.shape) for w in weights],
        out_specs=[row(w.shape[1]) for w in weights],
        out_shape=[jax.ShapeDtypeStruct((t, w.shape[1]), dt) for w, dt in zip(weights, out_dtypes)],
        compiler_params=_cparams("parallel"),
        name="norm_proj",
    )(h, nw, *weights)


def _proj_res_kernel(h_ref, nw_ref, *refs):
    n_in = (len(refs) - 1) // 2
    o_ref = refs[-1]
    m = None
    for x_ref, w_ref in zip(refs[:n_in], refs[n_in:2 * n_in]):
        part = _dot(x_ref[...], w_ref[...])
        m = part if m is None else m + part
    o_ref[...] = h_ref[...] + _rms(m, nw_ref[...])


def _proj_res(h, nw, xs, weights):
    t, d = h.shape
    tm = min(ROW_TILE, t)
    row = lambda n: pl.BlockSpec((tm, n), lambda i: (i, 0))
    return pl.pallas_call(
        _proj_res_kernel,
        grid=(t // tm,),
        in_specs=[row(d), _const_spec((1, d))] + [row(x.shape[1]) for x in xs] + [_const_spec(w.shape) for w in weights],
        out_specs=row(d),
        out_shape=jax.ShapeDtypeStruct((t, d), F32),
        compiler_params=_cparams("parallel"),
        name="proj_res",
    )(h, nw, *xs, *weights)


def _attn_kernel(*refs, first, last):
    q_ref, kp_ref, kc_ref, vp_ref, vc_ref = refs[:5]
    if first:
        acc_o, st_o = refs[5:]
    elif last:
        acc_i, st_i, o_ref = refs[5:]
    else:
        acc_i, st_i, acc_o, st_o = refs[5:]
    blk = q_ref.shape[0]
    n = pl.program_id(2)
    ri = lax.broadcasted_iota(jnp.int32, (blk, blk), 0)
    ci = lax.broadcasted_iota(jnp.int32, (blk, blk), 1)
    prev_valid = ci >= ri
    cur_valid = ci <= ri
    no_prev = jnp.where(n > 0, 0.0, NEG_BIG)
    heads = range(A_HEADS)
    sls = [slice(hd * A_HEAD_DIM, (hd + 1) * A_HEAD_DIM) for hd in heads]
    sps = [jnp.where(prev_valid, _dot_nt(q_ref[:, sl], kp_ref[:, sl]), NEG_BIG) + no_prev for sl in sls]
    scs = [jnp.where(cur_valid, _dot_nt(q_ref[:, sl], kc_ref[:, sl]), NEG_BIG) for sl in sls]
    ms = [jnp.maximum(jnp.max(sp, axis=-1, keepdims=True), jnp.max(sc, axis=-1, keepdims=True))
          for sp, sc in zip(sps, scs)]
    pps = [jnp.exp(sp - m) for sp, m in zip(sps, ms)]
    pcs = [jnp.exp(sc - m) for sc, m in zip(scs, ms)]
    dens = [jnp.sum(pp, axis=-1, keepdims=True) + jnp.sum(pc, axis=-1, keepdims=True) for pp, pc in zip(pps, pcs)]
    nums = [_dot(pp.astype(BF16), vp_ref[:, sl]) + _dot(pc.astype(BF16), vc_ref[:, sl])
            for pp, pc, sl in zip(pps, pcs, sls)]
    if not first:
        m_ins = [st_i[:, hd:hd + 1] for hd in heads]
        l_ins = [st_i[:, A_HEADS + hd:A_HEADS + hd + 1] for hd in heads]
        m_news = [jnp.maximum(m_in, m) for m_in, m in zip(m_ins, ms)]
        w_ins = [jnp.exp(m_in - m_new) for m_in, m_new in zip(m_ins, m_news)]
        w_curs = [jnp.exp(m - m_new) for m, m_new in zip(ms, m_news)]
        nums = [acc_i[:, sl] * w_in + num * w_cur for sl, w_in, num, w_cur in zip(sls, w_ins, nums, w_curs)]
        dens = [l_in * w_in + den * w_cur for l_in, w_in, den, w_cur in zip(l_ins, w_ins, dens, w_curs)]
        ms = m_news
    if last:
        for sl, num, den in zip(sls, nums, dens):
            o_ref[:, sl] = (num / den).astype(o_ref.dtype)
    else:
        lane = lax.broadcasted_iota(jnp.int32, (blk, LANES), 1)
        stats = jnp.zeros((blk, LANES), F32)
        for hd, sl in zip(heads, sls):
            acc_o[:, sl] = nums[hd]
            stats = jnp.where(lane == hd, ms[hd], stats)
            stats = jnp.where(lane == A_HEADS + hd, dens[hd], stats)
        st_o[...] = stats


def _attn_pattern(q, k, v, carry, dil, first, last, bsz, seq):
    length = seq // dil
    nb = length // A_BLOCK
    view = lambda t, width: t.reshape(bsz, length, dil * width)
    cur = lambda width: pl.BlockSpec((None, A_BLOCK, width), lambda b, r, n: (b, n, r))
    prev = lambda width: pl.BlockSpec((None, A_BLOCK, width), lambda b, r, n: (b, jnp.maximum(n - 1, 0), r))
    qv, kv, vv = view(q, A_WIDTH), view(k, A_WIDTH), view(v, A_WIDTH)
    operands = [qv, kv, kv, vv, vv]
    in_specs = [cur(A_WIDTH), prev(A_WIDTH), cur(A_WIDTH), prev(A_WIDTH), cur(A_WIDTH)]
    if not first:
        acc, stats = carry
        operands += [view(acc, A_WIDTH), view(stats, LANES)]
        in_specs += [cur(A_WIDTH), cur(LANES)]
    t = bsz * seq
    if last:
        out_specs = cur(A_WIDTH)
        out_shape = jax.ShapeDtypeStruct((bsz, length, dil * A_WIDTH), BF16)
    else:
        out_specs = [cur(A_WIDTH), cur(LANES)]
        out_shape = [jax.ShapeDtypeStruct((bsz, length, dil * A_WIDTH), F32),
                     jax.ShapeDtypeStruct((bsz, length, dil * LANES), F32)]
    out = pl.pallas_call(
        functools.partial(_attn_kernel, first=first, last=last),
        grid=(bsz, dil, nb),
        in_specs=in_specs,
        out_specs=out_specs,
        out_shape=out_shape,
        compiler_params=_cparams("parallel", "parallel", "arbitrary"),
        name=f"attn_d{dil}",
    )(*operands)
    if last:
        return out.reshape(t, A_WIDTH)
    return out[0].reshape(t, A_WIDTH), out[1].reshape(t, LANES)


def _dilated_attention(q, k, v, bsz, seq):
    carry = None
    for i, dil in enumerate(A_DILATIONS):
        carry = _attn_pattern(q, k, v, carry, dil, i == 0, i == len(A_DILATIONS) - 1, bsz, seq)
    return carry


def _causal_conv(x_ref, w_ref, xbuf, first_step):
    rows = x_ref.shape[0]
    k_len = w_ref.shape[0]

    @pl.when(first_step)
    def _():
        xbuf[0:CONV_TAIL, :] = jnp.zeros((CONV_TAIL, xbuf.shape[1]), F32)

    xbuf[CONV_TAIL:CONV_TAIL + rows, :] = x_ref[...]
    y = None
    for k in range(k_len):
        term = w_ref[k:k + 1, :] * xbuf[pl.ds(CONV_TAIL - k_len + 1 + k, rows), :]
        y = term if y is None else y + term
    xbuf[0:CONV_TAIL, :] = xbuf[rows:rows + CONV_TAIL, :]
    return y


def _split_bf16(a):
    hi = a.astype(BF16)
    lo = (a - hi.astype(F32)).astype(BF16)
    return hi, lo


def _dot_split(a, b):
    a_hi, a_lo = a
    b_hi, b_lo = b
    lhs = jnp.concatenate([a_hi, a_lo], axis=1)
    rhs = jnp.concatenate([b_hi, b_hi], axis=0)
    return _dot(lhs, rhs) + _dot(a_hi, b_lo)


def _unit_lower_inverses(a_list, eye):
    ps = [_split_bf16(-a) for a in a_list]
    xs = [eye - a for a in a_list]
    for _ in range(int(math.log2(eye.shape[0])) - 1):
        ps = [_split_bf16(_dot_split(p, p)) for p in ps]
        xs = [x + _dot_split(_split_bf16(x), p) for x, p in zip(xs, ps)]
    return xs


def _gdn_kernel(qkv_ref, z_ref, sm_ref, cw_ref, alog_ref, dtb_ref, hnw_ref, o_ref, xbuf, state):
    rows = qkv_ref.shape[0]
    first_step = pl.program_id(1) == 0
    y = _silu(_causal_conv(qkv_ref, cw_ref, xbuf, first_step))

    @pl.when(first_step)
    def _():
        state[...] = jnp.zeros(state.shape, F32)

    sm = sm_ref[...]
    beta = jax.nn.sigmoid(sm)
    g = -jnp.exp(alog_ref[...]) * _softplus(sm + dtb_ref[...])
    ri = lax.broadcasted_iota(jnp.int32, (GDN_CHUNK, GDN_CHUNK), 0)
    ci = lax.broadcasted_iota(jnp.int32, (GDN_CHUNK, GDN_CHUNK), 1)
    tri = ci <= ri
    strict = ci < ri
    eye = jnp.where(ci == ri, 1.0, 0.0)
    ones_tri = jnp.where(tri, 1.0, 0.0)
    hnw = hnw_ref[...]
    n_chunks = rows // GDN_CHUNK
    a_mats, qks, rhss, q_decs, k_decs, keeps = [], [], [], [], [], []
    for c in range(n_chunks):
        rs = slice(c * GDN_CHUNK, (c + 1) * GDN_CHUNK)
        gc = _dot(ones_tri, g[rs], HIGHEST)
        gc_t = gc.T
        g_last = gc[GDN_CHUNK - 1:GDN_CHUNK, :]
        e_gc = jnp.exp(gc)
        e_rem = jnp.exp(g_last - gc)
        e_last = jnp.exp(g_last)
        for hd in range(B_HEADS):
            lane = B_HEADS + hd
            col = lambda base: slice(base + hd * B_HEAD_DIM, base + (hd + 1) * B_HEAD_DIM)
            qh, kh, vh = y[rs, col(0)], y[rs, col(B_WIDTH)], y[rs, col(2 * B_WIDTH)]
            qh = qh * (lax.rsqrt(jnp.sum(qh * qh, axis=-1, keepdims=True) + NORM_EPS) * B_HEAD_DIM ** -0.5)
            kh = kh * lax.rsqrt(jnp.sum(kh * kh, axis=-1, keepdims=True) + NORM_EPS)
            b_col = beta[rs, hd:hd + 1]
            e_col = e_gc[:, lane:lane + 1]
            decay = jnp.exp(jnp.where(tri, gc[:, lane:lane + 1] - gc_t[lane:lane + 1, :], NEG_BIG))
            kb = kh.astype(BF16)
            qk_kk = _dot_nt(jnp.concatenate([qh.astype(BF16), kb], axis=0), kb)
            qks.append(jnp.where(tri, qk_kk[:GDN_CHUNK] * decay, 0.0).astype(BF16))
            a_mats.append(jnp.where(strict, qk_kk[GDN_CHUNK:] * decay, 0.0) * b_col)
            rhss.append(_split_bf16(jnp.concatenate([vh * b_col, kh * (b_col * e_col)], axis=1)))
            q_decs.append((qh * e_col).astype(BF16))
            k_decs.append((kh * e_rem[:, lane:lane + 1]).astype(BF16))
            keeps.append(e_last[:, lane:lane + 1])
    t_invs = _unit_lower_inverses(a_mats, eye)
    uws = [_dot_split(_split_bf16(t), rhs) for t, rhs in zip(t_invs, rhss)]
    for c in range(n_chunks):
        rs = slice(c * GDN_CHUNK, (c + 1) * GDN_CHUNK)
        for hd in range(B_HEADS):
            i = c * B_HEADS + hd
            col = slice(hd * B_HEAD_DIM, (hd + 1) * B_HEAD_DIM)
            uw = uws[i]
            s = state[hd]
            ws_qs = _dot(jnp.concatenate([uw[:, B_HEAD_DIM:].astype(BF16), q_decs[i]], axis=0), s.astype(BF16))
            vb = (uw[:, :B_HEAD_DIM] - ws_qs[:GDN_CHUNK]).astype(BF16)
            o = ws_qs[GDN_CHUNK:] + _dot(qks[i], vb)
            state[hd] = s * keeps[i] + _dot_tn(k_decs[i], vb)
            o = _rms(o, hnw) * _silu(z_ref[rs, col])
            o_ref[rs, col] = o.astype(o_ref.dtype)


def _gdn(qkv, z, sm, conv_w, alog_row, dtb_row, hnw, bsz, seq):
    rows = GDN_CHUNKS_PER_STEP * GDN_CHUNK
    steps = seq // rows
    width = qkv.shape[1]
    row = lambda n: pl.BlockSpec((rows, n), lambda b, s: (b * steps + s, 0))
    return pl.pallas_call(
        _gdn_kernel,
        grid=(bsz, steps),
        in_specs=[row(width), row(B_WIDTH), row(LANES), _const_spec(conv_w.shape), _const_spec((1, LANES)),
                  _const_spec((1, LANES)), _const_spec((1, B_HEAD_DIM))],
        out_specs=row(B_WIDTH),
        out_shape=jax.ShapeDtypeStruct((bsz * seq, B_WIDTH), BF16),
        scratch_shapes=[pltpu.VMEM((rows + CONV_TAIL, width), F32),
                        pltpu.VMEM((B_HEADS, B_HEAD_DIM, B_HEAD_DIM), F32)],
        compiler_params=_cparams("parallel", "arbitrary"),
        name="gdn",
    )(qkv, z, sm, conv_w, alog_row, dtb_row, hnw)


def _ssd_kernel(xbc_ref, z_ref, dt_ref, cw_ref, cb_ref, dtb_ref, alog_ref, dsk_ref, onw_ref, o_ref, xbuf, state):
    rows = xbc_ref.shape[0]
    first_step = pl.program_id(1) == 0
    y = _silu(_causal_conv(xbc_ref, cw_ref, xbuf, first_step) + cb_ref[...])

    @pl.when(first_step)
    def _():
        state[...] = jnp.zeros(state.shape, F32)

    dt = _softplus(dt_ref[...] + dtb_ref[...])
    a = -jnp.exp(alog_ref[...]) * dt
    ri = lax.broadcasted_iota(jnp.int32, (rows, rows), 0)
    ci = lax.broadcasted_iota(jnp.int32, (rows, rows), 1)
    tri = ci <= ri
    acum = _dot(jnp.where(tri, 1.0, 0.0), a, HIGHEST)
    acum_t = acum.T
    a_last = acum[rows - 1:rows, :]
    e_acum = jnp.exp(acum)
    e_rem = jnp.exp(a_last - acum)
    e_last = jnp.exp(a_last)
    dsk = dsk_ref[...]
    onw = onw_ref[...]
    for g in range(C_GROUPS):
        b_g = y[:, C_D_INNER + g * C_STATE:C_D_INNER + (g + 1) * C_STATE].astype(BF16)
        c_g = y[:, C_D_INNER + (C_GROUPS + g) * C_STATE:C_D_INNER + (C_GROUPS + g + 1) * C_STATE].astype(BF16)
        cb = _dot_nt(c_g, b_g)
        st = state[g]
        y_off = _dot(c_g, st.astype(BF16))
        outs, xdecs, keeps = [], [], []
        for r in range(C_HEADS_PER_GROUP):
            hd = g * C_HEADS_PER_GROUP + r
            sl = slice(hd * C_HEAD_DIM, (hd + 1) * C_HEAD_DIM)
            lsl = slice(r * C_HEAD_DIM, (r + 1) * C_HEAD_DIM)
            seg = jnp.exp(jnp.where(tri, acum[:, hd:hd + 1] - acum_t[hd:hd + 1, :], NEG_BIG))
            x_h = y[:, sl]
            xdt = x_h * dt[:, hd:hd + 1]
            y_h = _dot((cb * seg).astype(BF16), xdt.astype(BF16))
            y_h = y_h + y_off[:, lsl] * e_acum[:, hd:hd + 1] + dsk[:, sl] * x_h
            outs.append(y_h * _silu(z_ref[:, sl]))
            xdecs.append((xdt * e_rem[:, hd:hd + 1]).astype(BF16))
            keeps.append(jnp.broadcast_to(e_last[:, hd:hd + 1], (1, C_HEAD_DIM)))
        state[g] = st * jnp.concatenate(keeps, axis=1) + _dot_tn(b_g, jnp.concatenate(xdecs, axis=1))
        gsl = slice(g * C_GROUP_WIDTH, (g + 1) * C_GROUP_WIDTH)
        o_ref[:, gsl] = _rms(jnp.concatenate(outs, axis=1), onw[:, gsl]).astype(o_ref.dtype)


def _ssd(xbc, z, dt, conv_w, conv_b, dtb_row, alog_row, dsk_row, onw_row, bsz, seq):
    rows = SSD_CHUNK
    steps = seq // rows
    row = lambda n: pl.BlockSpec((rows, n), lambda b, s: (b * steps + s, 0))
    return pl.pallas_call(
        _ssd_kernel,
        grid=(bsz, steps),
        in_specs=[row(C_XBC), row(C_D_INNER), row(LANES), _const_spec(conv_w.shape), _const_spec((1, C_XBC)),
                  _const_spec((1, LANES)), _const_spec((1, LANES)), _const_spec((1, C_D_INNER)),
                  _const_spec((1, C_D_INNER))],
        out_specs=row(C_D_INNER),
        out_shape=jax.ShapeDtypeStruct((bsz * seq, C_D_INNER), BF16),
        scratch_shapes=[pltpu.VMEM((rows + CONV_TAIL, C_XBC), F32),
                        pltpu.VMEM((C_GROUPS, C_STATE, C_GROUP_WIDTH), F32)],
        compiler_params=_cparams("parallel", "arbitrary"),
        name="ssd",
    )(xbc, z, dt, conv_w, conv_b, dtb_row, alog_row, dsk_row, onw_row)


def _pad_lanes(x, offset=0):
    x = x.reshape(-1, x.shape[-1])
    return jnp.pad(x, ((0, 0), (offset, LANES - offset - x.shape[-1])))


def _even_mixer(h, pre_w, post_w, w_in, conv_w, a_log, dt_bias, head_norm_w, w_out, bsz, seq):
    cuts = [A_WIDTH, 2 * A_WIDTH, 3 * A_WIDTH, 3 * A_WIDTH + 3 * B_WIDTH, 3 * A_WIDTH + 4 * B_WIDTH]
    w_q = (w_in[:, :cuts[0]] * A_HEAD_DIM ** -0.5).astype(BF16)
    w_k = w_in[:, cuts[0]:cuts[1]].astype(BF16)
    w_v = w_in[:, cuts[1]:cuts[2]].astype(BF16)
    w_qkv = w_in[:, cuts[2]:cuts[3]].astype(BF16)
    w_z = w_in[:, cuts[3]:cuts[4]].astype(BF16)
    w_sm = jnp.pad(w_in[:, cuts[4]:], ((0, 0), (0, LANES - 2 * B_HEADS))).astype(BF16)
    qa, ka, va, qkv_b, z, sm = _norm_proj(h, pre_w, [w_q, w_k, w_v, w_qkv, w_z, w_sm],
                                          [BF16, BF16, BF16, F32, F32, F32])
    o_a = _dilated_attention(qa, ka, va, bsz, seq)
    o_b = _gdn(qkv_b, z, sm, conv_w, _pad_lanes(a_log, B_HEADS), _pad_lanes(dt_bias, B_HEADS),
               head_norm_w.reshape(1, -1), bsz, seq)
    w_o = w_out.astype(BF16)
    return _proj_res(h, post_w, [o_a, o_b], [w_o[:A_WIDTH], w_o[A_WIDTH:]])


def _odd_mixer(h, pre_w, post_w, w_in, conv_w, conv_b, dt_bias, a_log, d_skip, out_norm_w, w_out, bsz, seq):
    w_z = w_in[:, :C_D_INNER].astype(BF16)
    w_xbc = w_in[:, C_D_INNER:C_D_INNER + C_XBC].astype(BF16)
    w_dt = jnp.pad(w_in[:, C_D_INNER + C_XBC:], ((0, 0), (0, LANES - C_HEADS))).astype(BF16)
    z, xbc, dt = _norm_proj(h, pre_w, [w_z, w_xbc, w_dt], [F32, F32, F32])
    y = _ssd(xbc, z, dt, conv_w, conv_b.reshape(1, -1), _pad_lanes(dt_bias), _pad_lanes(a_log),
             jnp.repeat(d_skip, C_HEAD_DIM).reshape(1, -1), out_norm_w.reshape(1, -1), bsz, seq)
    return _proj_res(h, post_w, [y], [w_out.astype(BF16)])


def kernel(x, norm_w, ffn_w_gate, ffn_w_up, ffn_w_down, even_w_in, even_conv_w, even_a_log, even_dt_bias,
           even_head_norm_w, even_w_out, odd_w_in, odd_conv_w, odd_conv_b, odd_dt_bias, odd_a_log, odd_d_skip,
           odd_out_norm_w, odd_w_out):
    bsz, seq, d = x.shape
    depth = norm_w.shape[0]
    h = x.reshape(bsz * seq, d)
    wg, wu, wd = ffn_w_gate.astype(BF16), ffn_w_up.astype(BF16), ffn_w_down.astype(BF16)
    for layer in range(depth):
        nw = norm_w[layer][:, None, :]
        i = layer // 2
        h = _ffn(h, nw[0], nw[1], wg[layer, 0], wu[layer, 0], wd[layer, 0])
        if layer % 2 == 0:
            h = _even_mixer(h, nw[2], nw[3], even_w_in[i], even_conv_w[i], even_a_log[i], even_dt_bias[i],
                            even_head_norm_w[i], even_w_out[i], bsz, seq)
        else:
            h = _odd_mixer(h, nw[2], nw[3], odd_w_in[i], odd_conv_w[i], odd_conv_b[i], odd_dt_bias[i],
                           odd_a_log[i], odd_d_skip[i], odd_out_norm_w[i], odd_w_out[i], bsz, seq)
        h = _ffn(h, nw[4], nw[5], wg[layer, 1], wu[layer, 1], wd[layer, 1])
    return h.reshape(bsz, seq, d)
```

```python
import functools
import math

import jax
import jax.numpy as jnp
from jax import lax
from jax.experimental import pallas as pl
from jax.experimental.pallas import tpu as pltpu

F32 = jnp.float32
BF16 = jnp.bfloat16
HIGHEST = lax.Precision.HIGHEST

NORM_EPS = 1e-6
FFN_RES_SCALE = 0.5

A_HEADS = 8
A_HEAD_DIM = 64
A_WIDTH = A_HEADS * A_HEAD_DIM
A_DILATIONS = (1, 4, 16)
A_BLOCK = 128

B_HEADS = 4
B_HEAD_DIM = 128
B_WIDTH = B_HEADS * B_HEAD_DIM
B_CONV = 4

C_D_INNER = 2048
C_HEAD_DIM = 64
C_HEADS = C_D_INNER // C_HEAD_DIM
C_GROUPS = 4
C_STATE = 128
C_CONV = 4
C_XBC = C_D_INNER + 2 * C_GROUPS * C_STATE
C_GROUP_WIDTH = C_D_INNER // C_GROUPS
C_HEADS_PER_GROUP = C_HEADS // C_GROUPS

LANES = 128
CONV_TAIL = 8
NEG_BIG = -1e30

A_SLABS = A_WIDTH // LANES
ATTN_TOKENS = A_BLOCK * max(A_DILATIONS)
ATTN_SLABS_PER_STEP = 2

ROW_TILE = 512
GDN_CHUNK = 128
GDN_CHUNKS_PER_STEP = 2
SSD_CHUNK = 128
VMEM_LIMIT = 56 * 1024 * 1024


def _cparams(*sem):
    return pltpu.CompilerParams(dimension_semantics=sem, vmem_limit_bytes=VMEM_LIMIT)


def _dot(a, b, precision=None):
    return jnp.dot(a, b, preferred_element_type=F32, precision=precision)


def _dot_nt(a, b, precision=None):
    return lax.dot_general(a, b, (((1,), (1,)), ((), ())), preferred_element_type=F32, precision=precision)


def _dot_tn(a, b, precision=None):
    return lax.dot_general(a, b, (((0,), (0,)), ((), ())), preferred_element_type=F32, precision=precision)


def _rms(x, w):
    return x * lax.rsqrt(jnp.mean(x * x, axis=-1, keepdims=True) + NORM_EPS) * w


def _silu(x):
    return x * jax.nn.sigmoid(x)


def _softplus(x):
    return jnp.maximum(x, 0.0) + jnp.log1p(jnp.exp(-jnp.abs(x)))


def _const_spec(shape):
    return pl.BlockSpec(shape, lambda *_: (0,) * len(shape), pipeline_mode=pl.Buffered(1))


def _ffn_kernel(h_ref, pre_ref, post_ref, wg_ref, wu_ref, wd_ref, o_ref):
    x = h_ref[...]
    xn = _rms(x, pre_ref[...]).astype(BF16)
    gate = _dot(xn, wg_ref[...])
    up = _dot(xn, wu_ref[...])
    act = (_silu(gate) * up).astype(BF16)
    f = _dot(act, wd_ref[...])
    o_ref[...] = x + FFN_RES_SCALE * _rms(f, post_ref[...])


def _ffn(h, pre_w, post_w, wg, wu, wd):
    t, d = h.shape
    ff = wg.shape[1]
    tm = min(ROW_TILE, t)
    row = pl.BlockSpec((tm, d), lambda i: (i, 0))
    return pl.pallas_call(
        _ffn_kernel,
        grid=(t // tm,),
        in_specs=[row, _const_spec((1, d)), _const_spec((1, d)), _const_spec((d, ff)), _const_spec((d, ff)),
                  _const_spec((ff, d))],
        out_specs=row,
        out_shape=jax.ShapeDtypeStruct((t, d), F32),
        compiler_params=_cparams("parallel"),
        name="ffn",
    )(h, pre_w, post_w, wg, wu, wd)


def _norm_proj_kernel(h_ref, nw_ref, *refs):
    n_out = len(refs) // 2
    xn = _rms(h_ref[...], nw_ref[...]).astype(BF16)
    for w_ref, o_ref in zip(refs[:n_out], refs[n_out:]):
        res = _dot(xn, w_ref[...]).astype(o_ref.dtype)
        if len(o_ref.shape) == 3:
            for s in range(o_ref.shape[0]):
                o_ref[s] = res[:, s * LANES:(s + 1) * LANES]
        else:
            o_ref[...] = res


def _norm_proj(h, nw, weights, out_dtypes, slabbed=()):
    t, d = h.shape
    tm = min(ROW_TILE, t)
    row = lambda n: pl.BlockSpec((tm, n), lambda i: (i, 0))
    slab = lambda n: pl.BlockSpec((n // LANES, tm, LANES), lambda i: (0, i, 0))
    widths = [w.shape[1] for w in weights]
    return pl.pallas_call(
        _norm_proj_kernel,
        grid=(t // tm,),
        in_specs=[row(d), _const_spec((1, d))] + [_const_spec(w.shape) for w in weights],
        out_specs=[slab(n) if i in slabbed else row(n) for i, n in enumerate(widths)],
        out_shape=[jax.ShapeDtypeStruct((n // LANES, t, LANES) if i in slabbed else (t, n), dt)
                   for i, (n, dt) in enumerate(zip(widths, out_dtypes))],
        compiler_params=_cparams("parallel"),
        name="norm_proj",
    )(h, nw, *weights)


def _proj_res_kernel(h_ref, nw_ref, *refs):
    n_in = (len(refs) - 1) // 2
    o_ref = refs[-1]
    m = None
    for x_ref, w_ref in zip(refs[:n_in], refs[n_in:2 * n_in]):
        if len(x_ref.shape) == 3:
            x = jnp.concatenate([x_ref[s] for s in range(x_ref.shape[0])], axis=1)
        else:
            x = x_ref[...]
        part = _dot(x, w_ref[...])
        m = part if m is None else m + part
    o_ref[...] = h_ref[...] + _rms(m, nw_ref[...])


def _proj_res(h, nw, xs, weights):
    t, d = h.shape
    tm = min(ROW_TILE, t)
    row = lambda n: pl.BlockSpec((tm, n), lambda i: (i, 0))
    x_spec = lambda x: (pl.BlockSpec((x.shape[0], tm, LANES), lambda i: (0, i, 0)) if x.ndim == 3
                        else row(x.shape[1]))
    return pl.pallas_call(
        _proj_res_kernel,
        grid=(t // tm,),
        in_specs=[row(d), _const_spec((1, d))] + [x_spec(x) for x in xs] + [_const_spec(w.shape) for w in weights],
        out_specs=row(d),
        out_shape=jax.ShapeDtypeStruct((t, d), F32),
        compiler_params=_cparams("parallel"),
        name="proj_res",
    )(h, nw, *xs, *weights)


def _attn_kernel(q_ref, kp_ref, kc_ref, vp_ref, vc_ref, o_ref, acc, m_run, l_run):
    n_slabs = q_ref.shape[0]
    tokens = q_ref.shape[1]
    has_prev_block = pl.program_id(2) > 0
    ri = lax.broadcasted_iota(jnp.int32, (A_BLOCK, A_BLOCK), 0)
    ci = lax.broadcasted_iota(jnp.int32, (A_BLOCK, A_BLOCK), 1)
    prev_valid = ci >= ri
    cur_valid = ci <= ri
    low_half = lax.broadcasted_iota(jnp.int32, (A_BLOCK, LANES), 1) < A_HEAD_DIM
    ones_b = jnp.ones((A_BLOCK, LANES), BF16)

    def block(dil, start, kprev_ref, vprev_ref, prev_start, prev_bias, first, last):
        rows = lambda s0: pl.ds(s0, A_BLOCK, stride=dil) if dil > 1 else pl.ds(s0, A_BLOCK)
        slabs = range(n_slabs)
        heads = range(2 * n_slabs)
        qm = []
        for s in slabs:
            qs = q_ref[s, rows(start), :]
            qm.append(jnp.where(low_half, qs, 0.0).astype(BF16))
            qm.append(jnp.where(low_half, 0.0, qs).astype(BF16))
        kc = [kc_ref[s, rows(start), :].astype(BF16) for s in slabs]
        kp = [kprev_ref[s, rows(prev_start), :].astype(BF16) for s in slabs]
        vc = [jnp.concatenate([vc_ref[s, rows(start), :].astype(BF16), ones_b], axis=1) for s in slabs]
        vp = [jnp.concatenate([vprev_ref[s, rows(prev_start), :].astype(BF16), ones_b], axis=1) for s in slabs]
        sps = [jnp.where(prev_valid, _dot_nt(qm[hd], kp[hd // 2]), NEG_BIG) + prev_bias for hd in heads]
        scs = [jnp.where(cur_valid, _dot_nt(qm[hd], kc[hd // 2]), NEG_BIG) for hd in heads]
        mbs = [jnp.broadcast_to(jnp.maximum(jnp.max(sp, axis=-1, keepdims=True), jnp.max(sc, axis=-1, keepdims=True)),
                                (A_BLOCK, LANES)) for sp, sc in zip(sps, scs)]
        pps = [jnp.exp(sp - mb).astype(BF16) for sp, mb in zip(sps, mbs)]
        pcs = [jnp.exp(sc - mb).astype(BF16) for sc, mb in zip(scs, mbs)]
        nds = [_dot(pp, vp[hd // 2]) + _dot(pc, vc[hd // 2]) for hd, (pp, pc) in enumerate(zip(pps, pcs))]
        for s in slabs:
            num = jnp.where(low_half, nds[2 * s][:, :LANES], nds[2 * s + 1][:, :LANES])
            den = jnp.where(low_half, nds[2 * s][:, LANES:], nds[2 * s + 1][:, LANES:])
            m = jnp.where(low_half, mbs[2 * s], mbs[2 * s + 1])
            if not first:
                m_in = m_run[s, rows(start), :]
                m_new = jnp.maximum(m_in, m)
                w_in = jnp.exp(m_in - m_new)
                w_cur = jnp.exp(m - m_new)
                num = acc[s, rows(start), :] * w_in + num * w_cur
                den = l_run[s, rows(start), :] * w_in + den * w_cur
                m = m_new
            if last:
                acc[s, rows(start), :] = num / den
            else:
                acc[s, rows(start), :] = num
                m_run[s, rows(start), :] = m
                l_run[s, rows(start), :] = den

    no_prev_bias = jnp.where(has_prev_block, 0.0, NEG_BIG)
    for p, dil in enumerate(A_DILATIONS):
        first, last = p == 0, p == len(A_DILATIONS) - 1
        sub = A_BLOCK * dil
        n_sub = tokens // sub

        def from_prev_block(r, carry, dil=dil, sub=sub, first=first, last=last):
            block(dil, r, kp_ref, vp_ref, tokens - sub + r, no_prev_bias, first, last)
            return carry

        def from_this_block(i, carry, dil=dil, sub=sub, first=first, last=last):
            start = (1 + i // dil) * sub + i % dil
            block(dil, start, kc_ref, vc_ref, start - sub, 0.0, first, last)
            return carry

        lax.fori_loop(0, dil, from_prev_block, 0)
        if n_sub > 1:
            lax.fori_loop(0, (n_sub - 1) * dil, from_this_block, 0)
    o_ref[...] = acc[...].astype(o_ref.dtype)


def _dilated_attention(q, k, v, bsz, seq):
    t = bsz * seq
    nblk = seq // ATTN_TOKENS
    shape = (ATTN_SLABS_PER_STEP, ATTN_TOKENS, LANES)
    cur = pl.BlockSpec(shape, lambda b, g, j: (g, b * nblk + j, 0))
    prev = pl.BlockSpec(shape, lambda b, g, j: (g, b * nblk + jnp.maximum(j - 1, 0), 0))
    return pl.pallas_call(
        _attn_kernel,
        grid=(bsz, A_SLABS // ATTN_SLABS_PER_STEP, nblk),
        in_specs=[cur, prev, cur, prev, cur],
        out_specs=cur,
        out_shape=jax.ShapeDtypeStruct((A_SLABS, t, LANES), BF16),
        scratch_shapes=[pltpu.VMEM(shape, F32)] * 3,
        compiler_params=_cparams("parallel", "parallel", "parallel"),
        name="attn",
    )(q, k, k, v, v)


def _causal_conv(x_ref, w_ref, xbuf, first_step):
    rows = x_ref.shape[0]
    k_len = w_ref.shape[0]

    @pl.when(first_step)
    def _():
        xbuf[0:CONV_TAIL, :] = jnp.zeros((CONV_TAIL, xbuf.shape[1]), F32)

    xbuf[CONV_TAIL:CONV_TAIL + rows, :] = x_ref[...]
    y = None
    for k in range(k_len):
        term = w_ref[k:k + 1, :] * xbuf[pl.ds(CONV_TAIL - k_len + 1 + k, rows), :]
        y = term if y is None else y + term
    xbuf[0:CONV_TAIL, :] = xbuf[rows:rows + CONV_TAIL, :]
    return y


def _split_bf16(a):
    hi = a.astype(BF16)
    lo = (a - hi.astype(F32)).astype(BF16)
    return hi, lo


def _dot_split(a, b):
    a_hi, a_lo = a
    b_hi, b_lo = b
    lhs = jnp.concatenate([a_hi, a_lo], axis=1)
    rhs = jnp.concatenate([b_hi, b_hi], axis=0)
    return _dot(lhs, rhs) + _dot(a_hi, b_lo)


def _unit_lower_inverses(a_list, eye):
    ps = [_split_bf16(-a) for a in a_list]
    xs = [eye - a for a in a_list]
    for _ in range(int(math.log2(eye.shape[0])) - 1):
        ps = [_split_bf16(_dot_split(p, p)) for p in ps]
        xs = [x + _dot_split(_split_bf16(x), p) for x, p in zip(xs, ps)]
    return xs


def _gdn_kernel(qkv_ref, z_ref, sm_ref, cw_ref, alog_ref, dtb_ref, hnw_ref, o_ref, xbuf, state):
    rows = qkv_ref.shape[0]
    first_step = pl.program_id(1) == 0
    y = _silu(_causal_conv(qkv_ref, cw_ref, xbuf, first_step))

    @pl.when(first_step)
    def _():
        state[...] = jnp.zeros(state.shape, F32)

    sm = sm_ref[...]
    beta = jax.nn.sigmoid(sm)
    g = -jnp.exp(alog_ref[...]) * _softplus(sm + dtb_ref[...])
    ri = lax.broadcasted_iota(jnp.int32, (GDN_CHUNK, GDN_CHUNK), 0)
    ci = lax.broadcasted_iota(jnp.int32, (GDN_CHUNK, GDN_CHUNK), 1)
    tri = ci <= ri
    strict = ci < ri
    eye = jnp.where(ci == ri, 1.0, 0.0)
    ones_tri = jnp.where(tri, 1.0, 0.0)
    hnw = hnw_ref[...]
    n_chunks = rows // GDN_CHUNK
    a_mats, qks, rhss, q_decs, k_decs, keeps = [], [], [], [], [], []
    for c in range(n_chunks):
        rs = slice(c * GDN_CHUNK, (c + 1) * GDN_CHUNK)
        gc = _dot(ones_tri, g[rs], HIGHEST)
        gc_t = gc.T
        g_last = gc[GDN_CHUNK - 1:GDN_CHUNK, :]
        e_gc = jnp.exp(gc)
        e_rem = jnp.exp(g_last - gc)
        e_last = jnp.exp(g_last)
        for hd in range(B_HEADS):
            lane = B_HEADS + hd
            col = lambda base: slice(base + hd * B_HEAD_DIM, base + (hd + 1) * B_HEAD_DIM)
            qh, kh, vh = y[rs, col(0)], y[rs, col(B_WIDTH)], y[rs, col(2 * B_WIDTH)]
            qh = qh * (lax.rsqrt(jnp.sum(qh * qh, axis=-1, keepdims=True) + NORM_EPS) * B_HEAD_DIM ** -0.5)
            kh = kh * lax.rsqrt(jnp.sum(kh * kh, axis=-1, keepdims=True) + NORM_EPS)
            b_col = beta[rs, hd:hd + 1]
            e_col = e_gc[:, lane:lane + 1]
            decay = jnp.exp(jnp.where(tri, gc[:, lane:lane + 1] - gc_t[lane:lane + 1, :], NEG_BIG))
            kb = kh.astype(BF16)
            qk_kk = _dot_nt(jnp.concatenate([qh.astype(BF16), kb], axis=0), kb)
            qks.append(jnp.where(tri, qk_kk[:GDN_CHUNK] * decay, 0.0).astype(BF16))
            a_mats.append(jnp.where(strict, qk_kk[GDN_CHUNK:] * decay, 0.0) * b_col)
            rhss.append(_split_bf16(jnp.concatenate([vh * b_col, kh * (b_col * e_col)], axis=1)))
            q_decs.append((qh * e_col).astype(BF16))
            k_decs.append((kh * e_rem[:, lane:lane + 1]).astype(BF16))
            keeps.append(e_last[:, lane:lane + 1])
    t_invs = _unit_lower_inverses(a_mats, eye)
    uws = [_dot_split(_split_bf16(t), rhs) for t, rhs in zip(t_invs, rhss)]
    for c in range(n_chunks):
        rs = slice(c * GDN_CHUNK, (c + 1) * GDN_CHUNK)
        for hd in range(B_HEADS):
            i = c * B_HEADS + hd
            col = slice(hd * B_HEAD_DIM, (hd + 1) * B_HEAD_DIM)
            uw = uws[i]
            s = state[hd]
            ws_qs = _dot(jnp.concatenate([uw[:, B_HEAD_DIM:].astype(BF16), q_decs[i]], axis=0), s.astype(BF16))
            vb = (uw[:, :B_HEAD_DIM] - ws_qs[:GDN_CHUNK]).astype(BF16)
            o = ws_qs[GDN_CHUNK:] + _dot(qks[i], vb)
            state[hd] = s * keeps[i] + _dot_tn(k_decs[i], vb)
            o = _rms(o, hnw) * _silu(z_ref[rs, col])
            o_ref[rs, col] = o.astype(o_ref.dtype)


def _gdn(qkv, z, sm, conv_w, alog_row, dtb_row, hnw, bsz, seq):
    rows = GDN_CHUNKS_PER_STEP * GDN_CHUNK
    steps = seq // rows
    width = qkv.shape[1]
    row = lambda n: pl.BlockSpec((rows, n), lambda b, s: (b * steps + s, 0))
    return pl.pallas_call(
        _gdn_kernel,
        grid=(bsz, steps),
        in_specs=[row(width), row(B_WIDTH), row(LANES), _const_spec(conv_w.shape), _const_spec((1, LANES)),
                  _const_spec((1, LANES)), _const_spec((1, B_HEAD_DIM))],
        out_specs=row(B_WIDTH),
        out_shape=jax.ShapeDtypeStruct((bsz * seq, B_WIDTH), BF16),
        scratch_shapes=[pltpu.VMEM((rows + CONV_TAIL, width), F32),
                        pltpu.VMEM((B_HEADS, B_HEAD_DIM, B_HEAD_DIM), F32)],
        compiler_params=_cparams("parallel", "arbitrary"),
        name="gdn",
    )(qkv, z, sm, conv_w, alog_row, dtb_row, hnw)


def _ssd_kernel(xbc_ref, z_ref, dt_ref, cw_ref, cb_ref, dtb_ref, alog_ref, dsk_ref, onw_ref, expand_ref, o_ref,
                xbuf, state):
    rows = xbc_ref.shape[0]
    first_step = pl.program_id(1) == 0
    y = _silu(_causal_conv(xbc_ref, cw_ref, xbuf, first_step) + cb_ref[...])

    @pl.when(first_step)
    def _():
        state[...] = jnp.zeros(state.shape, F32)

    dt = _softplus(dt_ref[...] + dtb_ref[...])
    a = -jnp.exp(alog_ref[...]) * dt
    ri = lax.broadcasted_iota(jnp.int32, (rows, rows), 0)
    ci = lax.broadcasted_iota(jnp.int32, (rows, rows), 1)
    tri = ci <= ri
    acum = _dot(jnp.where(tri, 1.0, 0.0), a, HIGHEST)
    acum_t = acum.T
    a_last = acum[rows - 1:rows, :]
    expand = expand_ref[...]
    spread = lambda cols: _dot(jnp.concatenate(_split_bf16(cols), axis=1), expand)
    dt_x = spread(dt)
    e_acum_x = spread(jnp.exp(acum))
    e_rem_x = spread(jnp.exp(a_last - acum))
    e_last_x = spread(jnp.broadcast_to(jnp.exp(a_last), (CONV_TAIL, LANES)))[0:1, :]
    xs = y[:, :C_D_INNER]
    xdt = xs * dt_x
    xdt_b = xdt.astype(BF16)
    xdec_b = (xdt * e_rem_x).astype(BF16)
    lane = lax.broadcasted_iota(jnp.int32, (rows, LANES), 1)
    low_half = lane < C_HEAD_DIM
    zero_b = jnp.zeros((rows, LANES), BF16)
    onw = onw_ref[...]
    b_gs = [y[:, C_D_INNER + g * C_STATE:C_D_INNER + (g + 1) * C_STATE].astype(BF16) for g in range(C_GROUPS)]
    c_gs = [y[:, C_D_INNER + (C_GROUPS + g) * C_STATE:C_D_INNER + (C_GROUPS + g + 1) * C_STATE].astype(BF16)
            for g in range(C_GROUPS)]
    cbs = [_dot_nt(c_g, b_g) for c_g, b_g in zip(c_gs, b_gs)]
    sts = [state[g] for g in range(C_GROUPS)]
    y_offs = [_dot(c_g, st.astype(BF16)) for c_g, st in zip(c_gs, sts)]
    segs = []
    for hd in range(C_HEADS):
        seg = jnp.exp(jnp.where(tri, acum[:, hd:hd + 1] - acum_t[hd:hd + 1, :], NEG_BIG))
        segs.append((cbs[hd // C_HEADS_PER_GROUP] * seg).astype(BF16))
    y_diags = []
    for pair in range(C_HEADS // 2):
        x_pair = xdt_b[:, pair * LANES:(pair + 1) * LANES]
        rhs = jnp.concatenate([jnp.where(low_half, x_pair, zero_b), jnp.where(low_half, zero_b, x_pair)], axis=0)
        y_diags.append(_dot(jnp.concatenate([segs[2 * pair], segs[2 * pair + 1]], axis=1), rhs))
    y_diag = jnp.concatenate(y_diags, axis=1)
    y_off = jnp.concatenate(y_offs, axis=1) * e_acum_x
    out = (y_diag + y_off + dsk_ref[...] * xs) * _silu(z_ref[...])
    for g in range(C_GROUPS):
        gsl = slice(g * C_GROUP_WIDTH, (g + 1) * C_GROUP_WIDTH)
        state[g] = sts[g] * e_last_x[:, gsl] + _dot_tn(b_gs[g], xdec_b[:, gsl])
        o_ref[:, gsl] = _rms(out[:, gsl], onw[:, gsl]).astype(o_ref.dtype)


def _ssd(xbc, z, dt, conv_w, conv_b, dtb_row, alog_row, dsk_row, onw_row, bsz, seq):
    rows = SSD_CHUNK
    steps = seq // rows
    row = lambda n: pl.BlockSpec((rows, n), lambda b, s: (b * steps + s, 0))
    head_of_channel = jnp.arange(C_D_INNER, dtype=jnp.int32) // C_HEAD_DIM
    expand = (jnp.arange(2 * LANES, dtype=jnp.int32)[:, None] % LANES == head_of_channel[None, :]).astype(BF16)
    return pl.pallas_call(
        _ssd_kernel,
        grid=(bsz, steps),
        in_specs=[row(C_XBC), row(C_D_INNER), row(LANES), _const_spec(conv_w.shape), _const_spec((1, C_XBC)),
                  _const_spec((1, LANES)), _const_spec((1, LANES)), _const_spec((1, C_D_INNER)),
                  _const_spec((1, C_D_INNER)), _const_spec((2 * LANES, C_D_INNER))],
        out_specs=row(C_D_INNER),
        out_shape=jax.ShapeDtypeStruct((bsz * seq, C_D_INNER), BF16),
        scratch_shapes=[pltpu.VMEM((rows + CONV_TAIL, C_XBC), F32),
                        pltpu.VMEM((C_GROUPS, C_STATE, C_GROUP_WIDTH), F32)],
        compiler_params=_cparams("parallel", "arbitrary"),
        name="ssd",
    )(xbc, z, dt, conv_w, conv_b, dtb_row, alog_row, dsk_row, onw_row, expand)


def _pad_lanes(x, offset=0):
    x = x.reshape(-1, x.shape[-1])
    return jnp.pad(x, ((0, 0), (offset, LANES - offset - x.shape[-1])))


def _even_mixer(h, pre_w, post_w, w_in, conv_w, a_log, dt_bias, head_norm_w, w_out, bsz, seq):
    cuts = [A_WIDTH, 2 * A_WIDTH, 3 * A_WIDTH, 3 * A_WIDTH + 3 * B_WIDTH, 3 * A_WIDTH + 4 * B_WIDTH]
    w_q = (w_in[:, :cuts[0]] * A_HEAD_DIM ** -0.5).astype(BF16)
    w_k = w_in[:, cuts[0]:cuts[1]].astype(BF16)
    w_v = w_in[:, cuts[1]:cuts[2]].astype(BF16)
    w_qkv = w_in[:, cuts[2]:cuts[3]].astype(BF16)
    w_z = w_in[:, cuts[3]:cuts[4]].astype(BF16)
    w_sm = jnp.pad(w_in[:, cuts[4]:], ((0, 0), (0, LANES - 2 * B_HEADS))).astype(BF16)
    qa, ka, va, qkv_b, z, sm = _norm_proj(h, pre_w, [w_q, w_k, w_v, w_qkv, w_z, w_sm],
                                          [F32, F32, F32, F32, F32, F32], slabbed=(0, 1, 2))
    o_a = _dilated_attention(qa, ka, va, bsz, seq)
    o_b = _gdn(qkv_b, z, sm, conv_w, _pad_lanes(a_log, B_HEADS), _pad_lanes(dt_bias, B_HEADS),
               head_norm_w.reshape(1, -1), bsz, seq)
    w_o = w_out.astype(BF16)
    return _proj_res(h, post_w, [o_a, o_b], [w_o[:A_WIDTH], w_o[A_WIDTH:]])


def _odd_mixer(h, pre_w, post_w, w_in, conv_w, conv_b, dt_bias, a_log, d_skip, out_norm_w, w_out, bsz, seq):
    w_z = w_in[:, :C_D_INNER].astype(BF16)
    w_xbc = w_in[:, C_D_INNER:C_D_INNER + C_XBC].astype(BF16)
    w_dt = jnp.pad(w_in[:, C_D_INNER + C_XBC:], ((0, 0), (0, LANES - C_HEADS))).astype(BF16)
    z, xbc, dt = _norm_proj(h, pre_w, [w_z, w_xbc, w_dt], [F32, F32, F32])
    y = _ssd(xbc, z, dt, conv_w, conv_b.reshape(1, -1), _pad_lanes(dt_bias), _pad_lanes(a_log),
             jnp.repeat(d_skip, C_HEAD_DIM).reshape(1, -1), out_norm_w.reshape(1, -1), bsz, seq)
    return _proj_res(h, post_w, [y], [w_out.astype(BF16)])


def kernel(x, norm_w, ffn_w_gate, ffn_w_up, ffn_w_down, even_w_in, even_conv_w, even_a_log, even_dt_bias,
           even_head_norm_w, even_w_out, odd_w_in, odd_conv_w, odd_conv_b, odd_dt_bias, odd_a_log, odd_d_skip,
           odd_out_norm_w, odd_w_out):
    bsz, seq, d = x.shape
    depth = norm_w.shape[0]
    h = x.reshape(bsz * seq, d)
    wg, wu, wd = ffn_w_gate.astype(BF16), ffn_w_up.astype(BF16), ffn_w_down.astype(BF16)
    for layer in range(depth):
        nw = norm_w[layer][:, None, :]
        i = layer // 2
        h = _ffn(h, nw[0], nw[1], wg[layer, 0], wu[layer, 0], wd[layer, 0])
        if layer % 2 == 0:
            h = _even_mixer(h, nw[2], nw[3], even_w_in[i], even_conv_w[i], even_a_log[i], even_dt_bias[i],
                            even_head_norm_w[i], even_w_out[i], bsz, seq)
        else:
            h = _odd_mixer(h, nw[2], nw[3], odd_w_in[i], odd_conv_w[i], odd_conv_b[i], odd_dt_bias[i],
                           odd_a_log[i], odd_d_skip[i], odd_out_norm_w[i], odd_w_out[i], bsz, seq)
        h = _ffn(h, nw[4], nw[5], wg[layer, 1], wu[layer, 1], wd[layer, 1])
    return h.reshape(bsz, seq, d)
```

```python
import functools
import math

import jax
import jax.numpy as jnp
from jax import lax
from jax.experimental import pallas as pl
from jax.experimental.pallas import tpu as pltpu

F32 = jnp.float32
BF16 = jnp.bfloat16
HIGHEST = lax.Precision.HIGHEST

NORM_EPS = 1e-6
FFN_RES_SCALE = 0.5

A_HEADS = 8
A_HEAD_DIM = 64
A_WIDTH = A_HEADS * A_HEAD_DIM
A_DILATIONS = (1, 4, 16)
A_BLOCK = 128

B_HEADS = 4
B_HEAD_DIM = 128
B_WIDTH = B_HEADS * B_HEAD_DIM
B_CONV = 4

C_D_INNER = 2048
C_HEAD_DIM = 64
C_HEADS = C_D_INNER // C_HEAD_DIM
C_GROUPS = 4
C_STATE = 128
C_CONV = 4
C_XBC = C_D_INNER + 2 * C_GROUPS * C_STATE
C_GROUP_WIDTH = C_D_INNER // C_GROUPS
C_HEADS_PER_GROUP = C_HEADS // C_GROUPS

LANES = 128
CONV_TAIL = 8
CONV_COLS = 512
NEG_BIG = -1e30

A_SLABS = A_WIDTH // LANES
ATTN_TOKENS = A_BLOCK * max(A_DILATIONS)
ATTN_SLABS_PER_STEP = 2
ATTN_BLOCKS_PER_TRIP = 4

ROW_TILE = 512
GDN_CHUNK = 128
GDN_CHUNKS_PER_STEP = 2
SSD_CHUNK = 128
VMEM_LIMIT = 56 * 1024 * 1024


def _cparams(*sem):
    return pltpu.CompilerParams(dimension_semantics=sem, vmem_limit_bytes=VMEM_LIMIT)


def _dot(a, b, precision=None):
    return jnp.dot(a, b, preferred_element_type=F32, precision=precision)


def _dot_nt(a, b, precision=None):
    return lax.dot_general(a, b, (((1,), (1,)), ((), ())), preferred_element_type=F32, precision=precision)


def _dot_tn(a, b, precision=None):
    return lax.dot_general(a, b, (((0,), (0,)), ((), ())), preferred_element_type=F32, precision=precision)


def _rms(x, w):
    return x * lax.rsqrt(jnp.mean(x * x, axis=-1, keepdims=True) + NORM_EPS) * w


def _silu(x):
    return x * jax.nn.sigmoid(x)


def _softplus(x):
    return jnp.maximum(x, 0.0) + jnp.log1p(jnp.exp(-jnp.abs(x)))


def _const_spec(shape):
    return pl.BlockSpec(shape, lambda *_: (0,) * len(shape), pipeline_mode=pl.Buffered(1))


def _ffn_kernel(h_ref, pre_ref, post_ref, wg_ref, wu_ref, wd_ref, o_ref):
    x = h_ref[...]
    xn = _rms(x, pre_ref[...]).astype(BF16)
    gate = _dot(xn, wg_ref[...])
    up = _dot(xn, wu_ref[...])
    act = (_silu(gate) * up).astype(BF16)
    f = _dot(act, wd_ref[...])
    o_ref[...] = x + FFN_RES_SCALE * _rms(f, post_ref[...])


def _ffn(h, pre_w, post_w, wg, wu, wd):
    t, d = h.shape
    ff = wg.shape[1]
    tm = min(ROW_TILE, t)
    row = pl.BlockSpec((tm, d), lambda i: (i, 0))
    return pl.pallas_call(
        _ffn_kernel,
        grid=(t // tm,),
        in_specs=[row, _const_spec((1, d)), _const_spec((1, d)), _const_spec((d, ff)), _const_spec((d, ff)),
                  _const_spec((ff, d))],
        out_specs=row,
        out_shape=jax.ShapeDtypeStruct((t, d), F32),
        compiler_params=_cparams("parallel"),
        name="ffn",
    )(h, pre_w, post_w, wg, wu, wd)


def _norm_proj_kernel(h_ref, hprev_ref, nw_ref, cw_ref, cb_ref, *refs, conv_index, tiles_per_seq):
    n_out = (len(refs) - 1) // 2
    cbuf = refs[-1]
    rows = h_ref.shape[0]
    nw = nw_ref[...]
    xn = _rms(h_ref[...], nw).astype(BF16)
    for idx, (w_ref, o_ref) in enumerate(zip(refs[:n_out], refs[n_out:2 * n_out])):
        if idx == conv_index:
            xp = _rms(hprev_ref[...], nw).astype(BF16)
            x_ext = jnp.concatenate([xp, xn], axis=0)
            keep_tail = jnp.where(pl.program_id(0) % tiles_per_seq == 0, 0.0, 1.0)
            k_len = cw_ref.shape[0]
            for c0 in range(0, cbuf.shape[1], CONV_COLS):
                cs = slice(c0, c0 + CONV_COLS)
                res = _dot(x_ext, w_ref[:, cs])
                cbuf[0:CONV_TAIL, cs] = res[0:CONV_TAIL] * keep_tail
                cbuf[CONV_TAIL:, cs] = res[CONV_TAIL:]
                y = cb_ref[:, cs]
                for k in range(k_len):
                    y = y + cw_ref[k:k + 1, cs] * cbuf[pl.ds(CONV_TAIL - k_len + 1 + k, rows), cs]
                o_ref[:, cs] = _silu(y).astype(o_ref.dtype)
            continue
        res = _dot(xn, w_ref[...]).astype(o_ref.dtype)
        if len(o_ref.shape) == 3:
            for s in range(o_ref.shape[0]):
                o_ref[s] = res[:, s * LANES:(s + 1) * LANES]
        else:
            o_ref[...] = res


def _norm_proj(h, nw, weights, out_dtypes, seq, conv_index, conv_w, conv_b, slabbed=()):
    t, d = h.shape
    tm = min(ROW_TILE, seq)
    row = lambda n: pl.BlockSpec((tm, n), lambda i: (i, 0))
    slab = lambda n: pl.BlockSpec((n // LANES, tm, LANES), lambda i: (0, i, 0))
    tail = pl.BlockSpec((CONV_TAIL, d), lambda i: (jnp.maximum(i * (tm // CONV_TAIL) - 1, 0), 0))
    widths = [w.shape[1] for w in weights]
    conv_width = widths[conv_index]
    return pl.pallas_call(
        functools.partial(_norm_proj_kernel, conv_index=conv_index, tiles_per_seq=seq // tm),
        grid=(t // tm,),
        in_specs=[row(d), tail, _const_spec((1, d)), _const_spec(conv_w.shape), _const_spec((1, conv_width))]
        + [_const_spec(w.shape) for w in weights],
        out_specs=[slab(n) if i in slabbed else row(n) for i, n in enumerate(widths)],
        out_shape=[jax.ShapeDtypeStruct((n // LANES, t, LANES) if i in slabbed else (t, n), dt)
                   for i, (n, dt) in enumerate(zip(widths, out_dtypes))],
        scratch_shapes=[pltpu.VMEM((tm + CONV_TAIL, conv_width), F32)],
        compiler_params=_cparams("parallel"),
        name="norm_proj",
    )(h, h, nw, conv_w, conv_b, *weights)


def _proj_res_kernel(h_ref, nw_ref, *refs):
    n_in = (len(refs) - 1) // 2
    o_ref = refs[-1]
    m = None
    for x_ref, w_ref in zip(refs[:n_in], refs[n_in:2 * n_in]):
        if len(x_ref.shape) == 3:
            x = jnp.concatenate([x_ref[s] for s in range(x_ref.shape[0])], axis=1)
        else:
            x = x_ref[...]
        part = _dot(x, w_ref[...])
        m = part if m is None else m + part
    o_ref[...] = h_ref[...] + _rms(m, nw_ref[...])


def _proj_res(h, nw, xs, weights):
    t, d = h.shape
    tm = min(ROW_TILE, t)
    row = lambda n: pl.BlockSpec((tm, n), lambda i: (i, 0))
    x_spec = lambda x: (pl.BlockSpec((x.shape[0], tm, LANES), lambda i: (0, i, 0)) if x.ndim == 3
                        else row(x.shape[1]))
    return pl.pallas_call(
        _proj_res_kernel,
        grid=(t // tm,),
        in_specs=[row(d), _const_spec((1, d))] + [x_spec(x) for x in xs] + [_const_spec(w.shape) for w in weights],
        out_specs=row(d),
        out_shape=jax.ShapeDtypeStruct((t, d), F32),
        compiler_params=_cparams("parallel"),
        name="proj_res",
    )(h, nw, *xs, *weights)


def _attn_kernel(q_ref, kp_ref, kc_ref, vp_ref, vc_ref, o_ref, acc, m_run, l_run):
    n_slabs = q_ref.shape[0]
    tokens = q_ref.shape[1]
    has_prev_block = pl.program_id(2) > 0
    ri = lax.broadcasted_iota(jnp.int32, (A_BLOCK, A_BLOCK), 0)
    ci = lax.broadcasted_iota(jnp.int32, (A_BLOCK, A_BLOCK), 1)
    prev_valid = ci >= ri
    cur_valid = ci <= ri
    low_half = lax.broadcasted_iota(jnp.int32, (A_BLOCK, LANES), 1) < A_HEAD_DIM
    ones_b = jnp.ones((A_BLOCK, LANES), BF16)

    def blocks(dil, starts, kprev_ref, vprev_ref, prev_starts, prev_bias, first, last):
        rows = lambda s0: pl.ds(s0, A_BLOCK, stride=dil) if dil > 1 else pl.ds(s0, A_BLOCK)
        units = [(st, pst, s) for st, pst in zip(starts, prev_starts) for s in range(n_slabs)]
        heads = range(2 * len(units))
        qm = []
        for st, _, s in units:
            qs = q_ref[s, rows(st), :]
            qm.append(jnp.where(low_half, qs, 0.0).astype(BF16))
            qm.append(jnp.where(low_half, 0.0, qs).astype(BF16))
        kc = [kc_ref[s, rows(st), :].astype(BF16) for st, _, s in units]
        kp = [kprev_ref[s, rows(pst), :].astype(BF16) for _, pst, s in units]
        vc = [jnp.concatenate([vc_ref[s, rows(st), :].astype(BF16), ones_b], axis=1) for st, _, s in units]
        vp = [jnp.concatenate([vprev_ref[s, rows(pst), :].astype(BF16), ones_b], axis=1) for _, pst, s in units]
        sps = [jnp.where(prev_valid, _dot_nt(qm[hd], kp[hd // 2]), NEG_BIG) + prev_bias for hd in heads]
        scs = [jnp.where(cur_valid, _dot_nt(qm[hd], kc[hd // 2]), NEG_BIG) for hd in heads]
        mbs = [jnp.broadcast_to(jnp.maximum(jnp.max(sp, axis=-1, keepdims=True), jnp.max(sc, axis=-1, keepdims=True)),
                                (A_BLOCK, LANES)) for sp, sc in zip(sps, scs)]
        pps = [jnp.exp(sp - mb).astype(BF16) for sp, mb in zip(sps, mbs)]
        pcs = [jnp.exp(sc - mb).astype(BF16) for sc, mb in zip(scs, mbs)]
        nds = [_dot(pp, vp[hd // 2]) + _dot(pc, vc[hd // 2]) for hd, (pp, pc) in enumerate(zip(pps, pcs))]
        for u, (st, _, s) in enumerate(units):
            num = jnp.where(low_half, nds[2 * u][:, :LANES], nds[2 * u + 1][:, :LANES])
            den = jnp.where(low_half, nds[2 * u][:, LANES:], nds[2 * u + 1][:, LANES:])
            m = jnp.where(low_half, mbs[2 * u], mbs[2 * u + 1])
            if not first:
                m_in = m_run[s, rows(st), :]
                m_new = jnp.maximum(m_in, m)
                w_in = jnp.exp(m_in - m_new)
                w_cur = jnp.exp(m - m_new)
                num = acc[s, rows(st), :] * w_in + num * w_cur
                den = l_run[s, rows(st), :] * w_in + den * w_cur
                m = m_new
            if last:
                acc[s, rows(st), :] = num / den
            else:
                acc[s, rows(st), :] = num
                m_run[s, rows(st), :] = m
                l_run[s, rows(st), :] = den

    def for_each_group(count, run):
        rem = count % ATTN_BLOCKS_PER_TRIP
        if rem:
            run(list(range(rem)))

        def trip(t, carry):
            run([rem + t * ATTN_BLOCKS_PER_TRIP + u for u in range(ATTN_BLOCKS_PER_TRIP)])
            return carry

        if count // ATTN_BLOCKS_PER_TRIP:
            lax.fori_loop(0, count // ATTN_BLOCKS_PER_TRIP, trip, 0)

    no_prev_bias = jnp.where(has_prev_block, 0.0, NEG_BIG)
    for p, dil in enumerate(A_DILATIONS):
        first, last = p == 0, p == len(A_DILATIONS) - 1
        sub = A_BLOCK * dil
        n_sub = tokens // sub

        def from_prev_block(rs, dil=dil, sub=sub, first=first, last=last):
            blocks(dil, rs, kp_ref, vp_ref, [tokens - sub + r for r in rs], no_prev_bias, first, last)

        def from_this_block(idx, dil=dil, sub=sub, first=first, last=last):
            starts = [(1 + i // dil) * sub + i % dil for i in idx]
            blocks(dil, starts, kc_ref, vc_ref, [st - sub for st in starts], 0.0, first, last)

        for_each_group(dil, from_prev_block)
        for_each_group((n_sub - 1) * dil, from_this_block)
    o_ref[...] = acc[...].astype(o_ref.dtype)


def _dilated_attention(q, k, v, bsz, seq):
    t = bsz * seq
    nblk = seq // ATTN_TOKENS
    shape = (ATTN_SLABS_PER_STEP, ATTN_TOKENS, LANES)
    cur = pl.BlockSpec(shape, lambda b, g, j: (g, b * nblk + j, 0))
    prev = pl.BlockSpec(shape, lambda b, g, j: (g, b * nblk + jnp.maximum(j - 1, 0), 0))
    return pl.pallas_call(
        _attn_kernel,
        grid=(bsz, A_SLABS // ATTN_SLABS_PER_STEP, nblk),
        in_specs=[cur, prev, cur, prev, cur],
        out_specs=cur,
        out_shape=jax.ShapeDtypeStruct((A_SLABS, t, LANES), BF16),
        scratch_shapes=[pltpu.VMEM(shape, F32)] * 3,
        compiler_params=_cparams("parallel", "parallel", "parallel"),
        name="attn",
    )(q, k, k, v, v)


def _split_bf16(a):
    hi = a.astype(BF16)
    lo = (a - hi.astype(F32)).astype(BF16)
    return hi, lo


def _dot_split(a, b):
    a_hi, a_lo = a
    b_hi, b_lo = b
    lhs = jnp.concatenate([a_hi, a_lo], axis=1)
    rhs = jnp.concatenate([b_hi, b_hi], axis=0)
    return _dot(lhs, rhs) + _dot(a_hi, b_lo)


def _unit_lower_inverses(a_list, eye):
    ps = [_split_bf16(-a) for a in a_list]
    xs = [eye - a for a in a_list]
    for _ in range(int(math.log2(eye.shape[0])) - 1):
        ps = [_split_bf16(_dot_split(p, p)) for p in ps]
        xs = [x + _dot_split(_split_bf16(x), p) for x, p in zip(xs, ps)]
    return xs


def _gdn_kernel(qkv_ref, z_ref, sm_ref, alog_ref, dtb_ref, hnw_ref, o_ref, state):
    rows = qkv_ref.shape[0]
    y = qkv_ref[...]

    @pl.when(pl.program_id(1) == 0)
    def _():
        state[...] = jnp.zeros(state.shape, F32)

    sm = sm_ref[...]
    beta = jax.nn.sigmoid(sm)
    g = -jnp.exp(alog_ref[...]) * _softplus(sm + dtb_ref[...])
    ri = lax.broadcasted_iota(jnp.int32, (GDN_CHUNK, GDN_CHUNK), 0)
    ci = lax.broadcasted_iota(jnp.int32, (GDN_CHUNK, GDN_CHUNK), 1)
    tri = ci <= ri
    strict = ci < ri
    eye = jnp.where(ci == ri, 1.0, 0.0)
    ones_tri = jnp.where(tri, 1.0, 0.0)
    hnw = hnw_ref[...]
    n_chunks = rows // GDN_CHUNK
    a_mats, qks, rhss, q_decs, k_decs, keeps = [], [], [], [], [], []
    for c in range(n_chunks):
        rs = slice(c * GDN_CHUNK, (c + 1) * GDN_CHUNK)
        gc = _dot(ones_tri, g[rs], HIGHEST)
        gc_t = gc.T
        g_last = gc[GDN_CHUNK - 1:GDN_CHUNK, :]
        e_gc = jnp.exp(gc)
        e_rem = jnp.exp(g_last - gc)
        e_last = jnp.exp(g_last)
        for hd in range(B_HEADS):
            lane = B_HEADS + hd
            col = lambda base: slice(base + hd * B_HEAD_DIM, base + (hd + 1) * B_HEAD_DIM)
            qh, kh, vh = y[rs, col(0)], y[rs, col(B_WIDTH)], y[rs, col(2 * B_WIDTH)]
            qh = qh * (lax.rsqrt(jnp.sum(qh * qh, axis=-1, keepdims=True) + NORM_EPS) * B_HEAD_DIM ** -0.5)
            kh = kh * lax.rsqrt(jnp.sum(kh * kh, axis=-1, keepdims=True) + NORM_EPS)
            b_col = beta[rs, hd:hd + 1]
            e_col = e_gc[:, lane:lane + 1]
            decay = jnp.exp(jnp.where(tri, gc[:, lane:lane + 1] - gc_t[lane:lane + 1, :], NEG_BIG))
            kb = kh.astype(BF16)
            qk_kk = _dot_nt(jnp.concatenate([qh.astype(BF16), kb], axis=0), kb)
            qks.append(jnp.where(tri, qk_kk[:GDN_CHUNK] * decay, 0.0).astype(BF16))
            a_mats.append(jnp.where(strict, qk_kk[GDN_CHUNK:] * decay, 0.0) * b_col)
            rhss.append(_split_bf16(jnp.concatenate([vh * b_col, kh * (b_col * e_col)], axis=1)))
            q_decs.append((qh * e_col).astype(BF16))
            k_decs.append((kh * e_rem[:, lane:lane + 1]).astype(BF16))
            keeps.append(e_last[:, lane:lane + 1])
    t_invs = _unit_lower_inverses(a_mats, eye)
    uws = [_dot_split(_split_bf16(t), rhs) for t, rhs in zip(t_invs, rhss)]
    for c in range(n_chunks):
        rs = slice(c * GDN_CHUNK, (c + 1) * GDN_CHUNK)
        for hd in range(B_HEADS):
            i = c * B_HEADS + hd
            col = slice(hd * B_HEAD_DIM, (hd + 1) * B_HEAD_DIM)
            uw = uws[i]
            s = state[hd]
            ws_qs = _dot(jnp.concatenate([uw[:, B_HEAD_DIM:].astype(BF16), q_decs[i]], axis=0), s.astype(BF16))
            vb = (uw[:, :B_HEAD_DIM] - ws_qs[:GDN_CHUNK]).astype(BF16)
            o = ws_qs[GDN_CHUNK:] + _dot(qks[i], vb)
            state[hd] = s * keeps[i] + _dot_tn(k_decs[i], vb)
            o = _rms(o, hnw) * _silu(z_ref[rs, col])
            o_ref[rs, col] = o.astype(o_ref.dtype)


def _gdn(qkv, z, sm, alog_row, dtb_row, hnw, bsz, seq):
    rows = GDN_CHUNKS_PER_STEP * GDN_CHUNK
    steps = seq // rows
    width = qkv.shape[1]
    row = lambda n: pl.BlockSpec((rows, n), lambda b, s: (b * steps + s, 0))
    return pl.pallas_call(
        _gdn_kernel,
        grid=(bsz, steps),
        in_specs=[row(width), row(B_WIDTH), row(LANES), _const_spec((1, LANES)), _const_spec((1, LANES)),
                  _const_spec((1, B_HEAD_DIM))],
        out_specs=row(B_WIDTH),
        out_shape=jax.ShapeDtypeStruct((bsz * seq, B_WIDTH), BF16),
        scratch_shapes=[pltpu.VMEM((B_HEADS, B_HEAD_DIM, B_HEAD_DIM), F32)],
        compiler_params=_cparams("parallel", "arbitrary"),
        name="gdn",
    )(qkv, z, sm, alog_row, dtb_row, hnw)


def _ssd_kernel(xbc_ref, z_ref, dt_ref, dtb_ref, alog_ref, dsk_ref, onw_ref, expand_ref, o_ref, state):
    rows = xbc_ref.shape[0]
    y = xbc_ref[...]

    @pl.when(pl.program_id(1) == 0)
    def _():
        state[...] = jnp.zeros(state.shape, F32)

    dt = _softplus(dt_ref[...] + dtb_ref[...])
    a = -jnp.exp(alog_ref[...]) * dt
    ri = lax.broadcasted_iota(jnp.int32, (rows, rows), 0)
    ci = lax.broadcasted_iota(jnp.int32, (rows, rows), 1)
    tri = ci <= ri
    acum = _dot(jnp.where(tri, 1.0, 0.0), a, HIGHEST)
    acum_t = acum.T
    a_last = acum[rows - 1:rows, :]
    expand = expand_ref[...]
    spread = lambda cols: _dot(jnp.concatenate(_split_bf16(cols), axis=1), expand)
    dt_x = spread(dt)
    e_acum_x = spread(jnp.exp(acum))
    e_rem_x = spread(jnp.exp(a_last - acum))
    e_last_x = spread(jnp.broadcast_to(jnp.exp(a_last), (CONV_TAIL, LANES)))[0:1, :]
    xs = y[:, :C_D_INNER]
    xdt = xs * dt_x
    xdt_b = xdt.astype(BF16)
    xdec_b = (xdt * e_rem_x).astype(BF16)
    lane = lax.broadcasted_iota(jnp.int32, (rows, LANES), 1)
    low_half = lane < C_HEAD_DIM
    zero_b = jnp.zeros((rows, LANES), BF16)
    onw = onw_ref[...]
    b_gs = [y[:, C_D_INNER + g * C_STATE:C_D_INNER + (g + 1) * C_STATE].astype(BF16) for g in range(C_GROUPS)]
    c_gs = [y[:, C_D_INNER + (C_GROUPS + g) * C_STATE:C_D_INNER + (C_GROUPS + g + 1) * C_STATE].astype(BF16)
            for g in range(C_GROUPS)]
    cbs = [_dot_nt(c_g, b_g) for c_g, b_g in zip(c_gs, b_gs)]
    sts = [state[g] for g in range(C_GROUPS)]
    y_offs = [_dot(c_g, st.astype(BF16)) for c_g, st in zip(c_gs, sts)]
    segs = []
    for hd in range(C_HEADS):
        seg = jnp.exp(jnp.where(tri, acum[:, hd:hd + 1] - acum_t[hd:hd + 1, :], NEG_BIG))
        segs.append((cbs[hd // C_HEADS_PER_GROUP] * seg).astype(BF16))
    y_diags = []
    for pair in range(C_HEADS // 2):
        x_pair = xdt_b[:, pair * LANES:(pair + 1) * LANES]
        rhs = jnp.concatenate([jnp.where(low_half, x_pair, zero_b), jnp.where(low_half, zero_b, x_pair)], axis=0)
        y_diags.append(_dot(jnp.concatenate([segs[2 * pair], segs[2 * pair + 1]], axis=1), rhs))
    y_diag = jnp.concatenate(y_diags, axis=1)
    y_off = jnp.concatenate(y_offs, axis=1) * e_acum_x
    out = (y_diag + y_off + dsk_ref[...] * xs) * _silu(z_ref[...])
    for g in range(C_GROUPS):
        gsl = slice(g * C_GROUP_WIDTH, (g + 1) * C_GROUP_WIDTH)
        state[g] = sts[g] * e_last_x[:, gsl] + _dot_tn(b_gs[g], xdec_b[:, gsl])
        o_ref[:, gsl] = _rms(out[:, gsl], onw[:, gsl]).astype(o_ref.dtype)


def _ssd(xbc, z, dt, dtb_row, alog_row, dsk_row, onw_row, bsz, seq):
    rows = SSD_CHUNK
    steps = seq // rows
    row = lambda n: pl.BlockSpec((rows, n), lambda b, s: (b * steps + s, 0))
    head_of_channel = jnp.arange(C_D_INNER, dtype=jnp.int32) // C_HEAD_DIM
    expand = (jnp.arange(2 * LANES, dtype=jnp.int32)[:, None] % LANES == head_of_channel[None, :]).astype(BF16)
    return pl.pallas_call(
        _ssd_kernel,
        grid=(bsz, steps),
        in_specs=[row(C_XBC), row(C_D_INNER), row(LANES),
                  _const_spec((1, LANES)), _const_spec((1, LANES)), _const_spec((1, C_D_INNER)),
                  _const_spec((1, C_D_INNER)), _const_spec((2 * LANES, C_D_INNER))],
        out_specs=row(C_D_INNER),
        out_shape=jax.ShapeDtypeStruct((bsz * seq, C_D_INNER), BF16),
        scratch_shapes=[pltpu.VMEM((C_GROUPS, C_STATE, C_GROUP_WIDTH), F32)],
        compiler_params=_cparams("parallel", "arbitrary"),
        name="ssd",
    )(xbc, z, dt, dtb_row, alog_row, dsk_row, onw_row, expand)


def _pad_lanes(x, offset=0):
    x = x.reshape(-1, x.shape[-1])
    return jnp.pad(x, ((0, 0), (offset, LANES - offset - x.shape[-1])))


def _even_mixer(h, pre_w, post_w, w_in, conv_w, a_log, dt_bias, head_norm_w, w_out, bsz, seq):
    cuts = [A_WIDTH, 2 * A_WIDTH, 3 * A_WIDTH, 3 * A_WIDTH + 3 * B_WIDTH, 3 * A_WIDTH + 4 * B_WIDTH]
    w_q = (w_in[:, :cuts[0]] * A_HEAD_DIM ** -0.5).astype(BF16)
    w_k = w_in[:, cuts[0]:cuts[1]].astype(BF16)
    w_v = w_in[:, cuts[1]:cuts[2]].astype(BF16)
    w_qkv = w_in[:, cuts[2]:cuts[3]].astype(BF16)
    w_z = w_in[:, cuts[3]:cuts[4]].astype(BF16)
    w_sm = jnp.pad(w_in[:, cuts[4]:], ((0, 0), (0, LANES - 2 * B_HEADS))).astype(BF16)
    no_bias = jnp.zeros((1, 3 * B_WIDTH), F32)
    qa, ka, va, qkv_b, z, sm = _norm_proj(h, pre_w, [w_q, w_k, w_v, w_qkv, w_z, w_sm],
                                          [F32, F32, F32, F32, F32, F32], seq, 3, conv_w, no_bias,
                                          slabbed=(0, 1, 2))
    o_a = _dilated_attention(qa, ka, va, bsz, seq)
    o_b = _gdn(qkv_b, z, sm, _pad_lanes(a_log, B_HEADS), _pad_lanes(dt_bias, B_HEADS),
               head_norm_w.reshape(1, -1), bsz, seq)
    w_o = w_out.astype(BF16)
    return _proj_res(h, post_w, [o_a, o_b], [w_o[:A_WIDTH], w_o[A_WIDTH:]])


def _odd_mixer(h, pre_w, post_w, w_in, conv_w, conv_b, dt_bias, a_log, d_skip, out_norm_w, w_out, bsz, seq):
    w_z = w_in[:, :C_D_INNER].astype(BF16)
    w_xbc = w_in[:, C_D_INNER:C_D_INNER + C_XBC].astype(BF16)
    w_dt = jnp.pad(w_in[:, C_D_INNER + C_XBC:], ((0, 0), (0, LANES - C_HEADS))).astype(BF16)
    z, xbc, dt = _norm_proj(h, pre_w, [w_z, w_xbc, w_dt], [F32, F32, F32], seq, 1, conv_w, conv_b.reshape(1, -1))
    y = _ssd(xbc, z, dt, _pad_lanes(dt_bias), _pad_lanes(a_log),
             jnp.repeat(d_skip, C_HEAD_DIM).reshape(1, -1), out_norm_w.reshape(1, -1), bsz, seq)
    return _proj_res(h, post_w, [y], [w_out.astype(BF16)])


def kernel(x, norm_w, ffn_w_gate, ffn_w_up, ffn_w_down, even_w_in, even_conv_w, even_a_log, even_dt_bias,
           even_head_norm_w, even_w_out, odd_w_in, odd_conv_w, odd_conv_b, odd_dt_bias, odd_a_log, odd_d_skip,
           odd_out_norm_w, odd_w_out):
    bsz, seq, d = x.shape
    depth = norm_w.shape[0]
    h = x.reshape(bsz * seq, d)
    wg, wu, wd = ffn_w_gate.astype(BF16), ffn_w_up.astype(BF16), ffn_w_down.astype(BF16)
    for layer in range(depth):
        nw = norm_w[layer][:, None, :]
        i = layer // 2
        h = _ffn(h, nw[0], nw[1], wg[layer, 0], wu[layer, 0], wd[layer, 0])
        if layer % 2 == 0:
            h = _even_mixer(h, nw[2], nw[3], even_w_in[i], even_conv_w[i], even_a_log[i], even_dt_bias[i],
                            even_head_norm_w[i], even_w_out[i], bsz, seq)
        else:
            h = _odd_mixer(h, nw[2], nw[3], odd_w_in[i], odd_conv_w[i], odd_conv_b[i], odd_dt_bias[i],
                           odd_a_log[i], odd_d_skip[i], odd_out_norm_w[i], odd_w_out[i], bsz, seq)
        h = _ffn(h, nw[4], nw[5], wg[layer, 1], wu[layer, 1], wd[layer, 1])
    return h.reshape(bsz, seq, d)
```

```python
import functools
import math

import jax
import jax.numpy as jnp
from jax import lax
from jax.experimental import pallas as pl
from jax.experimental.pallas import tpu as pltpu

F32 = jnp.float32
BF16 = jnp.bfloat16
HIGHEST = lax.Precision.HIGHEST

NORM_EPS = 1e-6
FFN_RES_SCALE = 0.5

A_HEADS = 8
A_HEAD_DIM = 64
A_WIDTH = A_HEADS * A_HEAD_DIM
A_DILATIONS = (16, 4, 1)
A_BLOCK = 128

B_HEADS = 4
B_HEAD_DIM = 128
B_WIDTH = B_HEADS * B_HEAD_DIM
B_CONV = 4

C_D_INNER = 2048
C_HEAD_DIM = 64
C_HEADS = C_D_INNER // C_HEAD_DIM
C_GROUPS = 4
C_STATE = 128
C_CONV = 4
C_XBC = C_D_INNER + 2 * C_GROUPS * C_STATE
C_GROUP_WIDTH = C_D_INNER // C_GROUPS
C_HEADS_PER_GROUP = C_HEADS // C_GROUPS

LANES = 128
CONV_TAIL = 8
CONV_COLS = 512
NEG_BIG = -1e30

A_SLABS = A_WIDTH // LANES
ATTN_TOKENS = A_BLOCK * max(A_DILATIONS)
ATTN_SLABS_PER_STEP = 2
ATTN_BLOCKS_PER_TRIP = 4

ROW_TILE = 512
GDN_CHUNK = 128
GDN_CHUNKS_PER_STEP = 2
SSD_CHUNK = 128
VMEM_LIMIT = 56 * 1024 * 1024


def _cparams(*sem):
    return pltpu.CompilerParams(dimension_semantics=sem, vmem_limit_bytes=VMEM_LIMIT)


def _dot(a, b, precision=None):
    return jnp.dot(a, b, preferred_element_type=F32, precision=precision)


def _dot_nt(a, b, precision=None):
    return lax.dot_general(a, b, (((1,), (1,)), ((), ())), preferred_element_type=F32, precision=precision)


def _dot_tn(a, b, precision=None):
    return lax.dot_general(a, b, (((0,), (0,)), ((), ())), preferred_element_type=F32, precision=precision)


def _rms(x, w):
    return x * lax.rsqrt(jnp.mean(x * x, axis=-1, keepdims=True) + NORM_EPS) * w


def _silu(x):
    return x * jax.nn.sigmoid(x)


def _softplus(x):
    return jnp.maximum(x, 0.0) + jnp.log1p(jnp.exp(-jnp.abs(x)))


def _const_spec(shape):
    return pl.BlockSpec(shape, lambda *_: (0,) * len(shape), pipeline_mode=pl.Buffered(1))


def _mixer_out(x_refs, w_refs):
    m = None
    for x_ref, w_ref in zip(x_refs, w_refs):
        if len(x_ref.shape) == 3:
            x = jnp.concatenate([x_ref[s] for s in range(x_ref.shape[0])], axis=1)
        else:
            x = x_ref[...]
        part = _dot(x, w_ref[...])
        m = part if m is None else m + part
    return m


def _ffn_kernel(h_ref, pre_ref, post_ref, wg_ref, wu_ref, wd_ref, *refs):
    o_ref = refs[-1]
    x = h_ref[...]
    if len(refs) > 1:
        n_in = (len(refs) - 2) // 2
        x = x + _rms(_mixer_out(refs[1:1 + n_in], refs[1 + n_in:1 + 2 * n_in]), refs[0][...])
    xn = _rms(x, pre_ref[...]).astype(BF16)
    gate = _dot(xn, wg_ref[...])
    up = _dot(xn, wu_ref[...])
    act = (_silu(gate) * up).astype(BF16)
    f = _dot(act, wd_ref[...])
    o_ref[...] = x + FFN_RES_SCALE * _rms(f, post_ref[...])


def _ffn(h, pre_w, post_w, wg, wu, wd, mixer=None):
    t, d = h.shape
    ff = wg.shape[1]
    tm = min(ROW_TILE, t)
    row = lambda n: pl.BlockSpec((tm, n), lambda i: (i, 0))
    operands = [h, pre_w, post_w, wg, wu, wd]
    in_specs = [row(d), _const_spec((1, d)), _const_spec((1, d)), _const_spec((d, ff)), _const_spec((d, ff)),
                _const_spec((ff, d))]
    if mixer is not None:
        nw, xs, weights = mixer
        operands += [nw, *xs, *weights]
        in_specs += [_const_spec((1, d))]
        in_specs += [pl.BlockSpec((x.shape[0], tm, LANES), lambda i: (0, i, 0)) if x.ndim == 3 else row(x.shape[1])
                     for x in xs]
        in_specs += [_const_spec(w.shape) for w in weights]
    return pl.pallas_call(
        _ffn_kernel,
        grid=(t // tm,),
        in_specs=in_specs,
        out_specs=row(d),
        out_shape=jax.ShapeDtypeStruct((t, d), F32),
        compiler_params=_cparams("parallel"),
        name="ffn",
    )(*operands)


def _norm_proj_kernel(h_ref, hprev_ref, nw_ref, cw_ref, cb_ref, *refs, conv_index, tiles_per_seq):
    n_out = len(refs) // 2
    nw = nw_ref[...]
    xn = _rms(h_ref[...], nw).astype(BF16)
    x_ext = jnp.concatenate([_rms(hprev_ref[...], nw).astype(BF16), xn], axis=0)
    keep_tail = jnp.where(pl.program_id(0) % tiles_per_seq == 0, 0.0, 1.0)
    k_len = cw_ref.shape[0]
    cw_w, co_ref = refs[conv_index], refs[n_out + conv_index]

    def plain_piece(w_ref, o_ref, c0, width):
        res = _dot(xn, w_ref[:, c0:c0 + width]).astype(o_ref.dtype)
        if len(o_ref.shape) == 3:
            for s in range(width // LANES):
                o_ref[c0 // LANES + s] = res[:, s * LANES:(s + 1) * LANES]
        else:
            o_ref[:, c0:c0 + width] = res

    plain = [(refs[i], refs[n_out + i], c0, min(CONV_COLS, refs[i].shape[1]))
             for i in range(n_out) if i != conv_index for c0 in range(0, refs[i].shape[1], CONV_COLS)]
    conv_chunks = [slice(c0, c0 + CONV_COLS) for c0 in range(0, cw_w.shape[1], CONV_COLS)]
    per_chunk = -(-len(plain) // len(conv_chunks))
    res_next = _dot(x_ext, cw_w[:, conv_chunks[0]])
    for c, cs in enumerate(conv_chunks):
        res = res_next
        if c + 1 < len(conv_chunks):
            res_next = _dot(x_ext, cw_w[:, conv_chunks[c + 1]])
        for piece in plain[c * per_chunk:(c + 1) * per_chunk]:
            plain_piece(*piece)
        res = jnp.concatenate([res[0:CONV_TAIL] * keep_tail, res[CONV_TAIL:]], axis=0)
        y = cb_ref[:, cs] + cw_ref[k_len - 1:k_len, cs] * res[CONV_TAIL:]
        for shift in range(1, k_len):
            y = y + cw_ref[k_len - 1 - shift:k_len - shift, cs] * pltpu.roll(res, shift, axis=0)[CONV_TAIL:]
        co_ref[:, cs] = _silu(y).astype(co_ref.dtype)


def _norm_proj(h, nw, weights, out_dtypes, seq, conv_index, conv_w, conv_b, slabbed=()):
    t, d = h.shape
    tm = min(ROW_TILE, seq)
    row = lambda n: pl.BlockSpec((tm, n), lambda i: (i, 0))
    slab = lambda n: pl.BlockSpec((n // LANES, tm, LANES), lambda i: (0, i, 0))
    tail = pl.BlockSpec((CONV_TAIL, d), lambda i: (jnp.maximum(i * (tm // CONV_TAIL) - 1, 0), 0))
    widths = [w.shape[1] for w in weights]
    conv_width = widths[conv_index]
    return pl.pallas_call(
        functools.partial(_norm_proj_kernel, conv_index=conv_index, tiles_per_seq=seq // tm),
        grid=(t // tm,),
        in_specs=[row(d), tail, _const_spec((1, d)), _const_spec(conv_w.shape), _const_spec((1, conv_width))]
        + [_const_spec(w.shape) for w in weights],
        out_specs=[slab(n) if i in slabbed else row(n) for i, n in enumerate(widths)],
        out_shape=[jax.ShapeDtypeStruct((n // LANES, t, LANES) if i in slabbed else (t, n), dt)
                   for i, (n, dt) in enumerate(zip(widths, out_dtypes))],
        compiler_params=_cparams("parallel"),
        name="norm_proj",
    )(h, h, nw, conv_w, conv_b, *weights)


def _attn_kernel(q_ref, kp_ref, kc_ref, vp_ref, vc_ref, o_ref, acc, m_run, l_run):
    n_slabs = q_ref.shape[0]
    tokens = q_ref.shape[1]
    has_prev_block = pl.program_id(2) > 0
    ri = lax.broadcasted_iota(jnp.int32, (A_BLOCK, A_BLOCK), 0)
    ci = lax.broadcasted_iota(jnp.int32, (A_BLOCK, A_BLOCK), 1)
    prev_valid = ci >= ri
    cur_valid = ci <= ri
    low_half = lax.broadcasted_iota(jnp.int32, (A_BLOCK, LANES), 1) < A_HEAD_DIM
    ones_b = jnp.ones((A_BLOCK, LANES), BF16)

    def blocks(dil, starts, kprev_ref, vprev_ref, prev_starts, prev_bias, first, last):
        rows = lambda s0: pl.ds(s0, A_BLOCK, stride=dil) if dil > 1 else pl.ds(s0, A_BLOCK)
        units = [(st, pst, s) for st, pst in zip(starts, prev_starts) for s in range(n_slabs)]
        heads = range(2 * len(units))
        qm = []
        for st, _, s in units:
            qs = q_ref[s, rows(st), :]
            qm.append(jnp.where(low_half, qs, 0.0).astype(BF16))
            qm.append(jnp.where(low_half, 0.0, qs).astype(BF16))
        kc = [kc_ref[s, rows(st), :].astype(BF16) for st, _, s in units]
        kp = [kprev_ref[s, rows(pst), :].astype(BF16) for _, pst, s in units]
        vc = [jnp.concatenate([vc_ref[s, rows(st), :].astype(BF16), ones_b], axis=1) for st, _, s in units]
        vp = [jnp.concatenate([vprev_ref[s, rows(pst), :].astype(BF16), ones_b], axis=1) for _, pst, s in units]
        sps = [jnp.where(prev_valid, _dot_nt(qm[hd], kp[hd // 2]), NEG_BIG) + prev_bias for hd in heads]
        scs = [jnp.where(cur_valid, _dot_nt(qm[hd], kc[hd // 2]), NEG_BIG) for hd in heads]
        mbs = [jnp.broadcast_to(jnp.maximum(jnp.max(sp, axis=-1, keepdims=True), jnp.max(sc, axis=-1, keepdims=True)),
                                (A_BLOCK, LANES)) for sp, sc in zip(sps, scs)]
        pps = [jnp.exp(sp - mb).astype(BF16) for sp, mb in zip(sps, mbs)]
        pcs = [jnp.exp(sc - mb).astype(BF16) for sc, mb in zip(scs, mbs)]
        nds = [_dot(pp, vp[hd // 2]) + _dot(pc, vc[hd // 2]) for hd, (pp, pc) in enumerate(zip(pps, pcs))]
        for u, (st, _, s) in enumerate(units):
            num = jnp.where(low_half, nds[2 * u][:, :LANES], nds[2 * u + 1][:, :LANES])
            den = jnp.where(low_half, nds[2 * u][:, LANES:], nds[2 * u + 1][:, LANES:])
            m = jnp.where(low_half, mbs[2 * u], mbs[2 * u + 1])
            if not first:
                m_in = m_run[s, rows(st), :]
                m_new = jnp.maximum(m_in, m)
                w_in = jnp.exp(m_in - m_new)
                w_cur = jnp.exp(m - m_new)
                num = acc[s, rows(st), :] * w_in + num * w_cur
                den = l_run[s, rows(st), :] * w_in + den * w_cur
                m = m_new
            if last:
                acc[s, rows(st), :] = num / den
            else:
                acc[s, rows(st), :] = num
                m_run[s, rows(st), :] = m
                l_run[s, rows(st), :] = den

    def for_each_group(count, run):
        rem = count % ATTN_BLOCKS_PER_TRIP
        if rem:
            run(list(range(rem)))

        def trip(t, carry):
            run([rem + t * ATTN_BLOCKS_PER_TRIP + u for u in range(ATTN_BLOCKS_PER_TRIP)])
            return carry

        if count // ATTN_BLOCKS_PER_TRIP:
            lax.fori_loop(0, count // ATTN_BLOCKS_PER_TRIP, trip, 0)

    no_prev_bias = jnp.where(has_prev_block, 0.0, NEG_BIG)
    for p, dil in enumerate(A_DILATIONS):
        first, last = p == 0, p == len(A_DILATIONS) - 1
        sub = A_BLOCK * dil
        n_sub = tokens // sub

        def from_prev_block(rs, dil=dil, sub=sub, first=first, last=last):
            blocks(dil, rs, kp_ref, vp_ref, [tokens - sub + r for r in rs], no_prev_bias, first, last)

        def from_this_block(idx, dil=dil, sub=sub, first=first, last=last):
            starts = [(1 + i // dil) * sub + i % dil for i in idx]
            blocks(dil, starts, kc_ref, vc_ref, [st - sub for st in starts], 0.0, first, last)

        for_each_group(dil, from_prev_block)
        for_each_group((n_sub - 1) * dil, from_this_block)
    o_ref[...] = acc[...].astype(o_ref.dtype)


def _dilated_attention(q, k, v, bsz, seq):
    t = bsz * seq
    nblk = seq // ATTN_TOKENS
    shape = (ATTN_SLABS_PER_STEP, ATTN_TOKENS, LANES)
    cur = pl.BlockSpec(shape, lambda b, g, j: (g, b * nblk + j, 0))
    prev = pl.BlockSpec(shape, lambda b, g, j: (g, b * nblk + jnp.maximum(j - 1, 0), 0))
    return pl.pallas_call(
        _attn_kernel,
        grid=(bsz, A_SLABS // ATTN_SLABS_PER_STEP, nblk),
        in_specs=[cur, prev, cur, prev, cur],
        out_specs=cur,
        out_shape=jax.ShapeDtypeStruct((A_SLABS, t, LANES), BF16),
        scratch_shapes=[pltpu.VMEM(shape, F32)] * 3,
        compiler_params=_cparams("parallel", "parallel", "parallel"),
        name="attn",
    )(q, k, k, v, v)


def _split_bf16(a):
    hi = a.astype(BF16)
    lo = (a - hi.astype(F32)).astype(BF16)
    return hi, lo


def _dot_split(a, b):
    a_hi, a_lo = a
    b_hi, b_lo = b
    lhs = jnp.concatenate([a_hi, a_lo], axis=1)
    rhs = jnp.concatenate([b_hi, b_hi], axis=0)
    return _dot(lhs, rhs) + _dot(a_hi, b_lo)


def _unit_lower_inverses(a_list, eye):
    ps = [_split_bf16(-a) for a in a_list]
    xs = [eye - a for a in a_list]
    for _ in range(int(math.log2(eye.shape[0])) - 1):
        ps = [_split_bf16(_dot_split(p, p)) for p in ps]
        xs = [x + _dot_split(_split_bf16(x), p) for x, p in zip(xs, ps)]
    return xs


def _gdn_kernel(qkv_ref, z_ref, sm_ref, alog_ref, dtb_ref, hnw_ref, o_ref, state):
    rows = qkv_ref.shape[0]
    y = qkv_ref[...]

    @pl.when(pl.program_id(1) == 0)
    def _():
        state[...] = jnp.zeros(state.shape, F32)

    sm = sm_ref[...]
    beta = jax.nn.sigmoid(sm)
    g = -jnp.exp(alog_ref[...]) * _softplus(sm + dtb_ref[...])
    ri = lax.broadcasted_iota(jnp.int32, (GDN_CHUNK, GDN_CHUNK), 0)
    ci = lax.broadcasted_iota(jnp.int32, (GDN_CHUNK, GDN_CHUNK), 1)
    tri = ci <= ri
    strict = ci < ri
    eye = jnp.where(ci == ri, 1.0, 0.0)
    ones_tri = jnp.where(tri, 1.0, 0.0)
    hnw = hnw_ref[...]
    n_chunks = rows // GDN_CHUNK
    a_mats, qks, rhss, q_decs, k_decs, keeps = [], [], [], [], [], []
    for c in range(n_chunks):
        rs = slice(c * GDN_CHUNK, (c + 1) * GDN_CHUNK)
        gc = _dot(ones_tri, g[rs], HIGHEST)
        gc_t = gc.T
        g_last = gc[GDN_CHUNK - 1:GDN_CHUNK, :]
        e_gc = jnp.exp(gc)
        e_rem = jnp.exp(g_last - gc)
        e_last = jnp.exp(g_last)
        for hd in range(B_HEADS):
            lane = B_HEADS + hd
            col = lambda base: slice(base + hd * B_HEAD_DIM, base + (hd + 1) * B_HEAD_DIM)
            qh, kh, vh = y[rs, col(0)], y[rs, col(B_WIDTH)], y[rs, col(2 * B_WIDTH)]
            qh = qh * (lax.rsqrt(jnp.sum(qh * qh, axis=-1, keepdims=True) + NORM_EPS) * B_HEAD_DIM ** -0.5)
            kh = kh * lax.rsqrt(jnp.sum(kh * kh, axis=-1, keepdims=True) + NORM_EPS)
            b_col = beta[rs, hd:hd + 1]
            e_col = e_gc[:, lane:lane + 1]
            decay = jnp.exp(jnp.where(tri, gc[:, lane:lane + 1] - gc_t[lane:lane + 1, :], NEG_BIG))
            kb = kh.astype(BF16)
            qk_kk = _dot_nt(jnp.concatenate([qh.astype(BF16), kb], axis=0), kb)
            qks.append(jnp.where(tri, qk_kk[:GDN_CHUNK] * decay, 0.0).astype(BF16))
            a_mats.append(jnp.where(strict, qk_kk[GDN_CHUNK:] * decay, 0.0) * b_col)
            rhss.append(_split_bf16(jnp.concatenate([vh * b_col, kh * (b_col * e_col)], axis=1)))
            q_decs.append((qh * e_col).astype(BF16))
            k_decs.append((kh * e_rem[:, lane:lane + 1]).astype(BF16))
            keeps.append(e_last[:, lane:lane + 1])
    t_invs = _unit_lower_inverses(a_mats, eye)
    uws = [_dot_split(_split_bf16(t), rhs) for t, rhs in zip(t_invs, rhss)]
    for c in range(n_chunks):
        rs = slice(c * GDN_CHUNK, (c + 1) * GDN_CHUNK)
        for hd in range(B_HEADS):
            i = c * B_HEADS + hd
            col = slice(hd * B_HEAD_DIM, (hd + 1) * B_HEAD_DIM)
            uw = uws[i]
            s = state[hd]
            ws_qs = _dot(jnp.concatenate([uw[:, B_HEAD_DIM:].astype(BF16), q_decs[i]], axis=0), s.astype(BF16))
            vb = (uw[:, :B_HEAD_DIM] - ws_qs[:GDN_CHUNK]).astype(BF16)
            o = ws_qs[GDN_CHUNK:] + _dot(qks[i], vb)
            state[hd] = s * keeps[i] + _dot_tn(k_decs[i], vb)
            o = _rms(o, hnw) * _silu(z_ref[rs, col])
            o_ref[rs, col] = o.astype(o_ref.dtype)


def _gdn(qkv, z, sm, alog_row, dtb_row, hnw, bsz, seq):
    rows = GDN_CHUNKS_PER_STEP * GDN_CHUNK
    steps = seq // rows
    width = qkv.shape[1]
    row = lambda n: pl.BlockSpec((rows, n), lambda b, s: (b * steps + s, 0))
    return pl.pallas_call(
        _gdn_kernel,
        grid=(bsz, steps),
        in_specs=[row(width), row(B_WIDTH), row(LANES), _const_spec((1, LANES)), _const_spec((1, LANES)),
                  _const_spec((1, B_HEAD_DIM))],
        out_specs=row(B_WIDTH),
        out_shape=jax.ShapeDtypeStruct((bsz * seq, B_WIDTH), BF16),
        scratch_shapes=[pltpu.VMEM((B_HEADS, B_HEAD_DIM, B_HEAD_DIM), F32)],
        compiler_params=_cparams("parallel", "arbitrary"),
        name="gdn",
    )(qkv, z, sm, alog_row, dtb_row, hnw)


def _ssd_kernel(xbc_ref, z_ref, dt_ref, dtb_ref, alog_ref, dsk_ref, onw_ref, expand_ref, o_ref, state):
    rows = xbc_ref.shape[0]
    y = xbc_ref[...]

    @pl.when(pl.program_id(1) == 0)
    def _():
        state[...] = jnp.zeros(state.shape, F32)

    dt = _softplus(dt_ref[...] + dtb_ref[...])
    a = -jnp.exp(alog_ref[...]) * dt
    ri = lax.broadcasted_iota(jnp.int32, (rows, rows), 0)
    ci = lax.broadcasted_iota(jnp.int32, (rows, rows), 1)
    tri = ci <= ri
    acum = _dot(jnp.where(tri, 1.0, 0.0), a, HIGHEST)
    acum_t = acum.T
    a_last = acum[rows - 1:rows, :]
    expand = expand_ref[...]
    spread = lambda cols: _dot(jnp.concatenate(_split_bf16(cols), axis=1), expand)
    dt_x = spread(dt)
    e_acum_x = spread(jnp.exp(acum))
    e_rem_x = spread(jnp.exp(a_last - acum))
    e_last_x = spread(jnp.broadcast_to(jnp.exp(a_last), (CONV_TAIL, LANES)))[0:1, :]
    lane = lax.broadcasted_iota(jnp.int32, (rows, LANES), 1)
    low_half = lane < C_HEAD_DIM
    zero_b = jnp.zeros((rows, LANES), BF16)
    b_gs = [y[:, C_D_INNER + g * C_STATE:C_D_INNER + (g + 1) * C_STATE].astype(BF16) for g in range(C_GROUPS)]
    c_gs = [y[:, C_D_INNER + (C_GROUPS + g) * C_STATE:C_D_INNER + (C_GROUPS + g + 1) * C_STATE].astype(BF16)
            for g in range(C_GROUPS)]
    cbs = [_dot_nt(c_g, b_g) for c_g, b_g in zip(c_gs, b_gs)]
    sts = [state[g] for g in range(C_GROUPS)]
    y_offs = [_dot(c_g, st.astype(BF16)) for c_g, st in zip(c_gs, sts)]
    segs = []
    for hd in range(C_HEADS):
        seg = jnp.exp(jnp.where(tri, acum[:, hd:hd + 1] - acum_t[hd:hd + 1, :], NEG_BIG))
        segs.append((cbs[hd // C_HEADS_PER_GROUP] * seg).astype(BF16))
    pairs_per_group = C_HEADS_PER_GROUP // 2
    for g in range(C_GROUPS):
        outs, xdecs = [], []
        sumsq = jnp.zeros((rows, 1), F32)
        for q in range(pairs_per_group):
            pair = g * pairs_per_group + q
            blk = slice(pair * LANES, (pair + 1) * LANES)
            xs = y[:, blk]
            xdt = xs * dt_x[:, blk]
            xdt_b = xdt.astype(BF16)
            rhs = jnp.concatenate([jnp.where(low_half, xdt_b, zero_b), jnp.where(low_half, zero_b, xdt_b)], axis=0)
            y_diag = _dot(jnp.concatenate([segs[2 * pair], segs[2 * pair + 1]], axis=1), rhs)
            y_off = y_offs[g][:, q * LANES:(q + 1) * LANES] * e_acum_x[:, blk]
            out = (y_diag + y_off + dsk_ref[:, blk] * xs) * _silu(z_ref[:, blk])
            sumsq = sumsq + jnp.sum(out * out, axis=-1, keepdims=True)
            outs.append(out)
            xdecs.append((xdt * e_rem_x[:, blk]).astype(BF16))
        gsl = slice(g * C_GROUP_WIDTH, (g + 1) * C_GROUP_WIDTH)
        state[g] = sts[g] * e_last_x[:, gsl] + _dot_tn(b_gs[g], jnp.concatenate(xdecs, axis=1))
        scale = lax.rsqrt(sumsq * (1.0 / C_GROUP_WIDTH) + NORM_EPS)
        for q, out in enumerate(outs):
            blk = slice((g * pairs_per_group + q) * LANES, (g * pairs_per_group + q + 1) * LANES)
            o_ref[:, blk] = (out * scale * onw_ref[:, blk]).astype(o_ref.dtype)


def _ssd(xbc, z, dt, dtb_row, alog_row, dsk_row, onw_row, bsz, seq):
    rows = SSD_CHUNK
    steps = seq // rows
    row = lambda n: pl.BlockSpec((rows, n), lambda b, s: (b * steps + s, 0))
    head_of_channel = jnp.arange(C_D_INNER, dtype=jnp.int32) // C_HEAD_DIM
    expand = (jnp.arange(2 * LANES, dtype=jnp.int32)[:, None] % LANES == head_of_channel[None, :]).astype(BF16)
    return pl.pallas_call(
        _ssd_kernel,
        grid=(bsz, steps),
        in_specs=[row(C_XBC), row(C_D_INNER), row(LANES),
                  _const_spec((1, LANES)), _const_spec((1, LANES)), _const_spec((1, C_D_INNER)),
                  _const_spec((1, C_D_INNER)), _const_spec((2 * LANES, C_D_INNER))],
        out_specs=row(C_D_INNER),
        out_shape=jax.ShapeDtypeStruct((bsz * seq, C_D_INNER), BF16),
        scratch_shapes=[pltpu.VMEM((C_GROUPS, C_STATE, C_GROUP_WIDTH), F32)],
        compiler_params=_cparams("parallel", "arbitrary"),
        name="ssd",
    )(xbc, z, dt, dtb_row, alog_row, dsk_row, onw_row, expand)


def _pad_lanes(x, offset=0):
    x = x.reshape(-1, x.shape[-1])
    return jnp.pad(x, ((0, 0), (offset, LANES - offset - x.shape[-1])))


def _even_mixer(h, pre_w, post_w, w_in, conv_w, a_log, dt_bias, head_norm_w, w_out, bsz, seq):
    cuts = [A_WIDTH, 2 * A_WIDTH, 3 * A_WIDTH, 3 * A_WIDTH + 3 * B_WIDTH, 3 * A_WIDTH + 4 * B_WIDTH]
    w_q = (w_in[:, :cuts[0]] * A_HEAD_DIM ** -0.5).astype(BF16)
    w_k = w_in[:, cuts[0]:cuts[1]].astype(BF16)
    w_v = w_in[:, cuts[1]:cuts[2]].astype(BF16)
    w_qkv = w_in[:, cuts[2]:cuts[3]].astype(BF16)
    w_z = w_in[:, cuts[3]:cuts[4]].astype(BF16)
    w_sm = jnp.pad(w_in[:, cuts[4]:], ((0, 0), (0, LANES - 2 * B_HEADS))).astype(BF16)
    no_bias = jnp.zeros((1, 3 * B_WIDTH), F32)
    qa, ka, va, qkv_b, z, sm = _norm_proj(h, pre_w, [w_q, w_k, w_v, w_qkv, w_z, w_sm],
                                          [F32, F32, F32, F32, F32, F32], seq, 3, conv_w, no_bias,
                                          slabbed=(0, 1, 2))
    o_a = _dilated_attention(qa, ka, va, bsz, seq)
    o_b = _gdn(qkv_b, z, sm, _pad_lanes(a_log, B_HEADS), _pad_lanes(dt_bias, B_HEADS),
               head_norm_w.reshape(1, -1), bsz, seq)
    w_o = w_out.astype(BF16)
    return post_w, [o_a, o_b], [w_o[:A_WIDTH], w_o[A_WIDTH:]]


def _odd_mixer(h, pre_w, post_w, w_in, conv_w, conv_b, dt_bias, a_log, d_skip, out_norm_w, w_out, bsz, seq):
    w_z = w_in[:, :C_D_INNER].astype(BF16)
    w_xbc = w_in[:, C_D_INNER:C_D_INNER + C_XBC].astype(BF16)
    w_dt = jnp.pad(w_in[:, C_D_INNER + C_XBC:], ((0, 0), (0, LANES - C_HEADS))).astype(BF16)
    z, xbc, dt = _norm_proj(h, pre_w, [w_z, w_xbc, w_dt], [F32, F32, F32], seq, 1, conv_w, conv_b.reshape(1, -1))
    y = _ssd(xbc, z, dt, _pad_lanes(dt_bias), _pad_lanes(a_log),
             jnp.repeat(d_skip, C_HEAD_DIM).reshape(1, -1), out_norm_w.reshape(1, -1), bsz, seq)
    return post_w, [y], [w_out.astype(BF16)]


def kernel(x, norm_w, ffn_w_gate, ffn_w_up, ffn_w_down, even_w_in, even_conv_w, even_a_log, even_dt_bias,
           even_head_norm_w, even_w_out, odd_w_in, odd_conv_w, odd_conv_b, odd_dt_bias, odd_a_log, odd_d_skip,
           odd_out_norm_w, odd_w_out):
    bsz, seq, d = x.shape
    depth = norm_w.shape[0]
    h = x.reshape(bsz * seq, d)
    wg, wu, wd = ffn_w_gate.astype(BF16), ffn_w_up.astype(BF16), ffn_w_down.astype(BF16)
    for layer in range(depth):
        nw = norm_w[layer][:, None, :]
        i = layer // 2
        h = _ffn(h, nw[0], nw[1], wg[layer, 0], wu[layer, 0], wd[layer, 0])
        if layer % 2 == 0:
            mixer = _even_mixer(h, nw[2], nw[3], even_w_in[i], even_conv_w[i], even_a_log[i], even_dt_bias[i],
                                even_head_norm_w[i], even_w_out[i], bsz, seq)
        else:
            mixer = _odd_mixer(h, nw[2], nw[3], odd_w_in[i], odd_conv_w[i], odd_conv_b[i], odd_dt_bias[i],
                               odd_a_log[i], odd_d_skip[i], odd_out_norm_w[i], odd_w_out[i], bsz, seq)
        h = _ffn(h, nw[4], nw[5], wg[layer, 1], wu[layer, 1], wd[layer, 1], mixer=mixer)
    return h.reshape(bsz, seq, d)
```

```python
import functools
import math

import jax
import jax.numpy as jnp
from jax import lax
from jax.experimental import pallas as pl
from jax.experimental.pallas import tpu as pltpu

F32 = jnp.float32
BF16 = jnp.bfloat16
HIGHEST = lax.Precision.HIGHEST

NORM_EPS = 1e-6
FFN_RES_SCALE = 0.5

A_HEADS = 8
A_HEAD_DIM = 64
A_WIDTH = A_HEADS * A_HEAD_DIM
A_DILATIONS = (16, 4, 1)
A_BLOCK = 128

B_HEADS = 4
B_HEAD_DIM = 128
B_WIDTH = B_HEADS * B_HEAD_DIM
B_CONV = 4

C_D_INNER = 2048
C_HEAD_DIM = 64
C_HEADS = C_D_INNER // C_HEAD_DIM
C_GROUPS = 4
C_STATE = 128
C_CONV = 4
C_XBC = C_D_INNER + 2 * C_GROUPS * C_STATE
C_GROUP_WIDTH = C_D_INNER // C_GROUPS
C_HEADS_PER_GROUP = C_HEADS // C_GROUPS

LANES = 128
CONV_TAIL = 8
CONV_COLS = 512
NEG_BIG = -1e30

A_SLABS = A_WIDTH // LANES
ATTN_TOKENS = A_BLOCK * max(A_DILATIONS)
ATTN_SLABS_PER_STEP = 2
ATTN_BLOCKS_PER_TRIP = 4

ROW_TILE = 512
GDN_CHUNK = 128
GDN_CHUNKS_PER_STEP = 4
SSD_CHUNK = 128
VMEM_LIMIT = 56 * 1024 * 1024


def _cparams(*sem):
    return pltpu.CompilerParams(dimension_semantics=sem, vmem_limit_bytes=VMEM_LIMIT)


def _dot(a, b, precision=None):
    return jnp.dot(a, b, preferred_element_type=F32, precision=precision)


def _dot_nt(a, b, precision=None):
    return lax.dot_general(a, b, (((1,), (1,)), ((), ())), preferred_element_type=F32, precision=precision)


def _dot_tn(a, b, precision=None):
    return lax.dot_general(a, b, (((0,), (0,)), ((), ())), preferred_element_type=F32, precision=precision)


def _rms(x, w):
    return x * lax.rsqrt(jnp.mean(x * x, axis=-1, keepdims=True) + NORM_EPS) * w


def _silu(x):
    return x * jax.nn.sigmoid(x)


def _softplus(x):
    return jnp.maximum(x, 0.0) + jnp.log1p(jnp.exp(-jnp.abs(x)))


def _const_spec(shape):
    return pl.BlockSpec(shape, lambda *_: (0,) * len(shape), pipeline_mode=pl.Buffered(1))


def _mixer_out(x_refs, w_refs):
    m = None
    for x_ref, w_ref in zip(x_refs, w_refs):
        if len(x_ref.shape) == 3:
            x = jnp.concatenate([x_ref[s] for s in range(x_ref.shape[0])], axis=1)
        else:
            x = x_ref[...]
        part = _dot(x, w_ref[...])
        m = part if m is None else m + part
    return m


def _ffn_kernel(h_ref, pre_ref, post_ref, wg_ref, wu_ref, wd_ref, *refs):
    o_ref = refs[-1]
    x = h_ref[...]
    if len(refs) > 1:
        n_in = (len(refs) - 2) // 2
        x = x + _rms(_mixer_out(refs[1:1 + n_in], refs[1 + n_in:1 + 2 * n_in]), refs[0][...])
    xn = _rms(x, pre_ref[...]).astype(BF16)
    gate = _dot(xn, wg_ref[...])
    up = _dot(xn, wu_ref[...])
    act = (_silu(gate) * up).astype(BF16)
    f = _dot(act, wd_ref[...])
    o_ref[...] = x + FFN_RES_SCALE * _rms(f, post_ref[...])


def _ffn(h, pre_w, post_w, wg, wu, wd, mixer=None):
    t, d = h.shape
    ff = wg.shape[1]
    tm = min(ROW_TILE, t)
    row = lambda n: pl.BlockSpec((tm, n), lambda i: (i, 0))
    operands = [h, pre_w, post_w, wg, wu, wd]
    in_specs = [row(d), _const_spec((1, d)), _const_spec((1, d)), _const_spec((d, ff)), _const_spec((d, ff)),
                _const_spec((ff, d))]
    if mixer is not None:
        nw, xs, weights = mixer
        operands += [nw, *xs, *weights]
        in_specs += [_const_spec((1, d))]
        in_specs += [pl.BlockSpec((x.shape[0], tm, LANES), lambda i: (0, i, 0)) if x.ndim == 3 else row(x.shape[1])
                     for x in xs]
        in_specs += [_const_spec(w.shape) for w in weights]
    return pl.pallas_call(
        _ffn_kernel,
        grid=(t // tm,),
        in_specs=in_specs,
        out_specs=row(d),
        out_shape=jax.ShapeDtypeStruct((t, d), F32),
        compiler_params=_cparams("parallel"),
        name="ffn",
    )(*operands)


def _norm_proj_kernel(h_ref, hprev_ref, nw_ref, cw_ref, cb_ref, *refs, conv_index, tiles_per_seq):
    n_out = len(refs) // 2
    nw = nw_ref[...]
    xn = _rms(h_ref[...], nw).astype(BF16)
    x_ext = jnp.concatenate([_rms(hprev_ref[...], nw).astype(BF16), xn], axis=0)
    keep_tail = jnp.where(pl.program_id(0) % tiles_per_seq == 0, 0.0, 1.0)
    k_len = cw_ref.shape[0]
    cw_w, co_ref = refs[conv_index], refs[n_out + conv_index]

    def plain_piece(w_ref, o_ref, c0, width):
        res = _dot(xn, w_ref[:, c0:c0 + width]).astype(o_ref.dtype)
        if len(o_ref.shape) == 3:
            for s in range(width // LANES):
                o_ref[c0 // LANES + s] = res[:, s * LANES:(s + 1) * LANES]
        else:
            o_ref[:, c0:c0 + width] = res

    plain = [(refs[i], refs[n_out + i], c0, min(CONV_COLS, refs[i].shape[1]))
             for i in range(n_out) if i != conv_index for c0 in range(0, refs[i].shape[1], CONV_COLS)]
    conv_chunks = [slice(c0, c0 + CONV_COLS) for c0 in range(0, cw_w.shape[1], CONV_COLS)]
    per_chunk = -(-len(plain) // len(conv_chunks))
    res_next = _dot(x_ext, cw_w[:, conv_chunks[0]])
    for c, cs in enumerate(conv_chunks):
        res = res_next
        if c + 1 < len(conv_chunks):
            res_next = _dot(x_ext, cw_w[:, conv_chunks[c + 1]])
        for piece in plain[c * per_chunk:(c + 1) * per_chunk]:
            plain_piece(*piece)
        res = jnp.concatenate([res[0:CONV_TAIL] * keep_tail, res[CONV_TAIL:]], axis=0)
        y = cb_ref[:, cs] + cw_ref[k_len - 1:k_len, cs] * res[CONV_TAIL:]
        for shift in range(1, k_len):
            y = y + cw_ref[k_len - 1 - shift:k_len - shift, cs] * pltpu.roll(res, shift, axis=0)[CONV_TAIL:]
        co_ref[:, cs] = _silu(y).astype(co_ref.dtype)


def _norm_proj(h, nw, weights, out_dtypes, seq, conv_index, conv_w, conv_b, slabbed=()):
    t, d = h.shape
    tm = min(ROW_TILE, seq)
    row = lambda n: pl.BlockSpec((tm, n), lambda i: (i, 0))
    slab = lambda n: pl.BlockSpec((n // LANES, tm, LANES), lambda i: (0, i, 0))
    tail = pl.BlockSpec((CONV_TAIL, d), lambda i: (jnp.maximum(i * (tm // CONV_TAIL) - 1, 0), 0))
    widths = [w.shape[1] for w in weights]
    conv_width = widths[conv_index]
    return pl.pallas_call(
        functools.partial(_norm_proj_kernel, conv_index=conv_index, tiles_per_seq=seq // tm),
        grid=(t // tm,),
        in_specs=[row(d), tail, _const_spec((1, d)), _const_spec(conv_w.shape), _const_spec((1, conv_width))]
        + [_const_spec(w.shape) for w in weights],
        out_specs=[slab(n) if i in slabbed else row(n) for i, n in enumerate(widths)],
        out_shape=[jax.ShapeDtypeStruct((n // LANES, t, LANES) if i in slabbed else (t, n), dt)
                   for i, (n, dt) in enumerate(zip(widths, out_dtypes))],
        compiler_params=_cparams("parallel"),
        name="norm_proj",
    )(h, h, nw, conv_w, conv_b, *weights)


def _attn_kernel(q_ref, kp_ref, kc_ref, vp_ref, vc_ref, o_ref, acc, m_run, l_run):
    n_slabs = q_ref.shape[0]
    tokens = q_ref.shape[1]
    has_prev_block = pl.program_id(2) > 0
    ri = lax.broadcasted_iota(jnp.int32, (A_BLOCK, A_BLOCK), 0)
    ci = lax.broadcasted_iota(jnp.int32, (A_BLOCK, A_BLOCK), 1)
    prev_valid = ci >= ri
    cur_valid = ci <= ri
    low_half = lax.broadcasted_iota(jnp.int32, (A_BLOCK, LANES), 1) < A_HEAD_DIM
    ones_b = jnp.ones((A_BLOCK, LANES), BF16)

    def blocks(dil, starts, kprev_ref, vprev_ref, prev_starts, prev_bias, first, last):
        rows = lambda s0: pl.ds(s0, A_BLOCK, stride=dil) if dil > 1 else pl.ds(s0, A_BLOCK)
        units = [(st, pst, s) for st, pst in zip(starts, prev_starts) for s in range(n_slabs)]
        heads = range(2 * len(units))
        qm = []
        for st, _, s in units:
            qs = q_ref[s, rows(st), :]
            qm.append(jnp.where(low_half, qs, 0.0).astype(BF16))
            qm.append(jnp.where(low_half, 0.0, qs).astype(BF16))
        kc = [kc_ref[s, rows(st), :].astype(BF16) for st, _, s in units]
        kp = [kprev_ref[s, rows(pst), :].astype(BF16) for _, pst, s in units]
        vc = [jnp.concatenate([vc_ref[s, rows(st), :].astype(BF16), ones_b], axis=1) for st, _, s in units]
        vp = [jnp.concatenate([vprev_ref[s, rows(pst), :].astype(BF16), ones_b], axis=1) for _, pst, s in units]
        sps = [jnp.where(prev_valid, _dot_nt(qm[hd], kp[hd // 2]), NEG_BIG) + prev_bias for hd in heads]
        scs = [jnp.where(cur_valid, _dot_nt(qm[hd], kc[hd // 2]), NEG_BIG) for hd in heads]
        mbs = [jnp.broadcast_to(jnp.maximum(jnp.max(sp, axis=-1, keepdims=True), jnp.max(sc, axis=-1, keepdims=True)),
                                (A_BLOCK, LANES)) for sp, sc in zip(sps, scs)]
        pps = [jnp.exp(sp - mb).astype(BF16) for sp, mb in zip(sps, mbs)]
        pcs = [jnp.exp(sc - mb).astype(BF16) for sc, mb in zip(scs, mbs)]
        nds = [_dot(pp, vp[hd // 2]) + _dot(pc, vc[hd // 2]) for hd, (pp, pc) in enumerate(zip(pps, pcs))]
        for u, (st, _, s) in enumerate(units):
            num = jnp.where(low_half, nds[2 * u][:, :LANES], nds[2 * u + 1][:, :LANES])
            den = jnp.where(low_half, nds[2 * u][:, LANES:], nds[2 * u + 1][:, LANES:])
            m = jnp.where(low_half, mbs[2 * u], mbs[2 * u + 1])
            if not first:
                m_in = m_run[s, rows(st), :]
                m_new = jnp.maximum(m_in, m)
                w_in = jnp.exp(m_in - m_new)
                w_cur = jnp.exp(m - m_new)
                num = acc[s, rows(st), :] * w_in + num * w_cur
                den = l_run[s, rows(st), :] * w_in + den * w_cur
                m = m_new
            if last:
                acc[s, rows(st), :] = num / den
            else:
                acc[s, rows(st), :] = num
                m_run[s, rows(st), :] = m
                l_run[s, rows(st), :] = den

    def for_each_group(count, run):
        rem = count % ATTN_BLOCKS_PER_TRIP
        if rem:
            run(list(range(rem)))

        def trip(t, carry):
            run([rem + t * ATTN_BLOCKS_PER_TRIP + u for u in range(ATTN_BLOCKS_PER_TRIP)])
            return carry

        if count // ATTN_BLOCKS_PER_TRIP:
            lax.fori_loop(0, count // ATTN_BLOCKS_PER_TRIP, trip, 0)

    no_prev_bias = jnp.where(has_prev_block, 0.0, NEG_BIG)
    for p, dil in enumerate(A_DILATIONS):
        first, last = p == 0, p == len(A_DILATIONS) - 1
        sub = A_BLOCK * dil
        n_sub = tokens // sub

        def from_prev_block(rs, dil=dil, sub=sub, first=first, last=last):
            blocks(dil, rs, kp_ref, vp_ref, [tokens - sub + r for r in rs], no_prev_bias, first, last)

        def from_this_block(idx, dil=dil, sub=sub, first=first, last=last):
            starts = [(1 + i // dil) * sub + i % dil for i in idx]
            blocks(dil, starts, kc_ref, vc_ref, [st - sub for st in starts], 0.0, first, last)

        for_each_group(dil, from_prev_block)
        for_each_group((n_sub - 1) * dil, from_this_block)
    o_ref[...] = acc[...].astype(o_ref.dtype)


def _dilated_attention(q, k, v, bsz, seq):
    t = bsz * seq
    nblk = seq // ATTN_TOKENS
    shape = (ATTN_SLABS_PER_STEP, ATTN_TOKENS, LANES)
    cur = pl.BlockSpec(shape, lambda b, g, j: (g, b * nblk + j, 0))
    prev = pl.BlockSpec(shape, lambda b, g, j: (g, b * nblk + jnp.maximum(j - 1, 0), 0))
    return pl.pallas_call(
        _attn_kernel,
        grid=(bsz, A_SLABS // ATTN_SLABS_PER_STEP, nblk),
        in_specs=[cur, prev, cur, prev, cur],
        out_specs=cur,
        out_shape=jax.ShapeDtypeStruct((A_SLABS, t, LANES), BF16),
        scratch_shapes=[pltpu.VMEM(shape, F32)] * 3,
        compiler_params=_cparams("parallel", "parallel", "parallel"),
        name="attn",
    )(q, k, k, v, v)


def _split_bf16(a):
    hi = a.astype(BF16)
    lo = (a - hi.astype(F32)).astype(BF16)
    return hi, lo


def _unit_lower_inverses(a_list, eye, ri, ci):
    n = eye.shape[0]

    def joins(b):
        return ((ri // (2 * b)) == (ci // (2 * b))) & ((ri % (2 * b)) >= b) & ((ci % (2 * b)) < b)

    a_bs = [a.astype(BF16) for a in a_list]
    zero_b = jnp.zeros((n, n), BF16)
    xs = [eye - jnp.where(joins(1), a, 0.0) for a in a_list]
    b = 2
    while b < n:
        mask = joins(b)
        x_bs = [x.astype(BF16) for x in xs]
        ys = [_dot(xb, jnp.where(mask, ab, zero_b)).astype(BF16) for xb, ab in zip(x_bs, a_bs)]
        xs = [x - _dot(y, xb) for x, y, xb in zip(xs, ys, x_bs)]
        b *= 2
    return xs


def _gdn_kernel(qkv_ref, z_ref, sm_ref, alog_ref, dtb_ref, hnw_ref, o_ref, state):
    rows = qkv_ref.shape[0]
    y = qkv_ref[...]

    @pl.when(pl.program_id(1) == 0)
    def _():
        state[...] = jnp.zeros(state.shape, F32)

    sm = sm_ref[...]
    beta = jax.nn.sigmoid(sm)
    g = -jnp.exp(alog_ref[...]) * _softplus(sm + dtb_ref[...])
    ri = lax.broadcasted_iota(jnp.int32, (GDN_CHUNK, GDN_CHUNK), 0)
    ci = lax.broadcasted_iota(jnp.int32, (GDN_CHUNK, GDN_CHUNK), 1)
    tri = ci <= ri
    strict = ci < ri
    eye = jnp.where(ci == ri, 1.0, 0.0)
    ones_tri = jnp.where(tri, 1.0, 0.0)
    hnw = hnw_ref[...]
    n_chunks = rows // GDN_CHUNK
    a_mats, qks, rhss, q_decs, k_decs, keeps = [], [], [], [], [], []
    for c in range(n_chunks):
        rs = slice(c * GDN_CHUNK, (c + 1) * GDN_CHUNK)
        gc = _dot(ones_tri, g[rs], HIGHEST)
        gc_t = gc.T
        g_last = gc[GDN_CHUNK - 1:GDN_CHUNK, :]
        e_gc = jnp.exp(gc)
        e_rem = jnp.exp(g_last - gc)
        e_last = jnp.exp(g_last)
        for hd in range(B_HEADS):
            lane = B_HEADS + hd
            col = lambda base: slice(base + hd * B_HEAD_DIM, base + (hd + 1) * B_HEAD_DIM)
            qh, kh, vh = y[rs, col(0)], y[rs, col(B_WIDTH)], y[rs, col(2 * B_WIDTH)]
            qh = qh * (lax.rsqrt(jnp.sum(qh * qh, axis=-1, keepdims=True) + NORM_EPS) * B_HEAD_DIM ** -0.5)
            kh = kh * lax.rsqrt(jnp.sum(kh * kh, axis=-1, keepdims=True) + NORM_EPS)
            b_col = beta[rs, hd:hd + 1]
            e_col = e_gc[:, lane:lane + 1]
            decay = jnp.exp(jnp.where(tri, gc[:, lane:lane + 1] - gc_t[lane:lane + 1, :], NEG_BIG))
            kb = kh.astype(BF16)
            qk_kk = _dot_nt(jnp.concatenate([qh.astype(BF16), kb], axis=0), kb)
            qks.append(jnp.where(tri, qk_kk[:GDN_CHUNK] * decay, 0.0).astype(BF16))
            a_mats.append(jnp.where(strict, qk_kk[GDN_CHUNK:] * decay, 0.0) * b_col)
            rhss.append(jnp.concatenate([vh * b_col, kh * (b_col * e_col)], axis=1).astype(BF16))
            q_decs.append((qh * e_col).astype(BF16))
            k_decs.append((kh * e_rem[:, lane:lane + 1]).astype(BF16))
            keeps.append(e_last[:, lane:lane + 1])
    t_invs = _unit_lower_inverses(a_mats, eye, ri, ci)
    uws = [_dot(t.astype(BF16), rhs) for t, rhs in zip(t_invs, rhss)]
    for c in range(n_chunks):
        rs = slice(c * GDN_CHUNK, (c + 1) * GDN_CHUNK)
        for hd in range(B_HEADS):
            i = c * B_HEADS + hd
            col = slice(hd * B_HEAD_DIM, (hd + 1) * B_HEAD_DIM)
            uw = uws[i]
            s = state[hd]
            ws_qs = _dot(jnp.concatenate([uw[:, B_HEAD_DIM:].astype(BF16), q_decs[i]], axis=0), s.astype(BF16))
            vb = (uw[:, :B_HEAD_DIM] - ws_qs[:GDN_CHUNK]).astype(BF16)
            o = ws_qs[GDN_CHUNK:] + _dot(qks[i], vb)
            state[hd] = s * keeps[i] + _dot_tn(k_decs[i], vb)
            o = _rms(o, hnw) * _silu(z_ref[rs, col])
            o_ref[rs, col] = o.astype(o_ref.dtype)


def _gdn(qkv, z, sm, alog_row, dtb_row, hnw, bsz, seq):
    rows = GDN_CHUNKS_PER_STEP * GDN_CHUNK
    steps = seq // rows
    width = qkv.shape[1]
    row = lambda n: pl.BlockSpec((rows, n), lambda b, s: (b * steps + s, 0))
    return pl.pallas_call(
        _gdn_kernel,
        grid=(bsz, steps),
        in_specs=[row(width), row(B_WIDTH), row(LANES), _const_spec((1, LANES)), _const_spec((1, LANES)),
                  _const_spec((1, B_HEAD_DIM))],
        out_specs=row(B_WIDTH),
        out_shape=jax.ShapeDtypeStruct((bsz * seq, B_WIDTH), BF16),
        scratch_shapes=[pltpu.VMEM((B_HEADS, B_HEAD_DIM, B_HEAD_DIM), F32)],
        compiler_params=_cparams("parallel", "arbitrary"),
        name="gdn",
    )(qkv, z, sm, alog_row, dtb_row, hnw)


def _ssd_kernel(xbc_ref, z_ref, dt_ref, dtb_ref, alog_ref, dsk_ref, onw_ref, expand_ref, o_ref, state):
    rows = xbc_ref.shape[0]
    y = xbc_ref[...]

    @pl.when(pl.program_id(1) == 0)
    def _():
        state[...] = jnp.zeros(state.shape, F32)

    dt = _softplus(dt_ref[...] + dtb_ref[...])
    a = -jnp.exp(alog_ref[...]) * dt
    ri = lax.broadcasted_iota(jnp.int32, (rows, rows), 0)
    ci = lax.broadcasted_iota(jnp.int32, (rows, rows), 1)
    tri = ci <= ri
    acum = _dot(jnp.where(tri, 1.0, 0.0), a, HIGHEST)
    acum_t = acum.T
    a_last = acum[rows - 1:rows, :]
    expand = expand_ref[...]
    spread = lambda cols: _dot(jnp.concatenate(_split_bf16(cols), axis=1), expand)
    dt_x = spread(dt)
    e_acum_x = spread(jnp.exp(acum))
    e_rem_x = spread(jnp.exp(a_last - acum))
    e_last_x = spread(jnp.broadcast_to(jnp.exp(a_last), (CONV_TAIL, LANES)))[0:1, :]
    lane = lax.broadcasted_iota(jnp.int32, (rows, LANES), 1)
    low_half = lane < C_HEAD_DIM
    zero_b = jnp.zeros((rows, LANES), BF16)
    b_gs = [y[:, C_D_INNER + g * C_STATE:C_D_INNER + (g + 1) * C_STATE].astype(BF16) for g in range(C_GROUPS)]
    c_gs = [y[:, C_D_INNER + (C_GROUPS + g) * C_STATE:C_D_INNER + (C_GROUPS + g + 1) * C_STATE].astype(BF16)
            for g in range(C_GROUPS)]
    cbs = [_dot_nt(c_g, b_g) for c_g, b_g in zip(c_gs, b_gs)]
    sts = [state[g] for g in range(C_GROUPS)]
    y_offs = [_dot(c_g, st.astype(BF16)) for c_g, st in zip(c_gs, sts)]
    segs = []
    for hd in range(C_HEADS):
        seg = jnp.exp(jnp.where(tri, acum[:, hd:hd + 1] - acum_t[hd:hd + 1, :], NEG_BIG))
        segs.append((cbs[hd // C_HEADS_PER_GROUP] * seg).astype(BF16))
    pairs_per_group = C_HEADS_PER_GROUP // 2
    for g in range(C_GROUPS):
        outs, xdecs = [], []
        sumsq = jnp.zeros((rows, 1), F32)
        for q in range(pairs_per_group):
            pair = g * pairs_per_group + q
            blk = slice(pair * LANES, (pair + 1) * LANES)
            xs = y[:, blk]
            xdt = xs * dt_x[:, blk]
            xdt_b = xdt.astype(BF16)
            rhs = jnp.concatenate([jnp.where(low_half, xdt_b, zero_b), jnp.where(low_half, zero_b, xdt_b)], axis=0)
            y_diag = _dot(jnp.concatenate([segs[2 * pair], segs[2 * pair + 1]], axis=1), rhs)
            y_off = y_offs[g][:, q * LANES:(q + 1) * LANES] * e_acum_x[:, blk]
            out = (y_diag + y_off + dsk_ref[:, blk] * xs) * _silu(z_ref[:, blk])
            sumsq = sumsq + jnp.sum(out * out, axis=-1, keepdims=True)
            outs.append(out)
            xdecs.append((xdt * e_rem_x[:, blk]).astype(BF16))
        gsl = slice(g * C_GROUP_WIDTH, (g + 1) * C_GROUP_WIDTH)
        state[g] = sts[g] * e_last_x[:, gsl] + _dot_tn(b_gs[g], jnp.concatenate(xdecs, axis=1))
        scale = lax.rsqrt(sumsq * (1.0 / C_GROUP_WIDTH) + NORM_EPS)
        for q, out in enumerate(outs):
            blk = slice((g * pairs_per_group + q) * LANES, (g * pairs_per_group + q + 1) * LANES)
            o_ref[:, blk] = (out * scale * onw_ref[:, blk]).astype(o_ref.dtype)


def _ssd(xbc, z, dt, dtb_row, alog_row, dsk_row, onw_row, bsz, seq):
    rows = SSD_CHUNK
    steps = seq // rows
    row = lambda n: pl.BlockSpec((rows, n), lambda b, s: (b * steps + s, 0))
    head_of_channel = jnp.arange(C_D_INNER, dtype=jnp.int32) // C_HEAD_DIM
    expand = (jnp.arange(2 * LANES, dtype=jnp.int32)[:, None] % LANES == head_of_channel[None, :]).astype(BF16)
    return pl.pallas_call(
        _ssd_kernel,
        grid=(bsz, steps),
        in_specs=[row(C_XBC), row(C_D_INNER), row(LANES),
                  _const_spec((1, LANES)), _const_spec((1, LANES)), _const_spec((1, C_D_INNER)),
                  _const_spec((1, C_D_INNER)), _const_spec((2 * LANES, C_D_INNER))],
        out_specs=row(C_D_INNER),
        out_shape=jax.ShapeDtypeStruct((bsz * seq, C_D_INNER), BF16),
        scratch_shapes=[pltpu.VMEM((C_GROUPS, C_STATE, C_GROUP_WIDTH), F32)],
        compiler_params=_cparams("parallel", "arbitrary"),
        name="ssd",
    )(xbc, z, dt, dtb_row, alog_row, dsk_row, onw_row, expand)


def _pad_lanes(x, offset=0):
    x = x.reshape(-1, x.shape[-1])
    return jnp.pad(x, ((0, 0), (offset, LANES - offset - x.shape[-1])))


def _even_mixer(h, pre_w, post_w, w_in, conv_w, a_log, dt_bias, head_norm_w, w_out, bsz, seq):
    cuts = [A_WIDTH, 2 * A_WIDTH, 3 * A_WIDTH, 3 * A_WIDTH + 3 * B_WIDTH, 3 * A_WIDTH + 4 * B_WIDTH]
    w_q = (w_in[:, :cuts[0]] * A_HEAD_DIM ** -0.5).astype(BF16)
    w_k = w_in[:, cuts[0]:cuts[1]].astype(BF16)
    w_v = w_in[:, cuts[1]:cuts[2]].astype(BF16)
    w_qkv = w_in[:, cuts[2]:cuts[3]].astype(BF16)
    w_z = w_in[:, cuts[3]:cuts[4]].astype(BF16)
    w_sm = jnp.pad(w_in[:, cuts[4]:], ((0, 0), (0, LANES - 2 * B_HEADS))).astype(BF16)
    no_bias = jnp.zeros((1, 3 * B_WIDTH), F32)
    qa, ka, va, qkv_b, z, sm = _norm_proj(h, pre_w, [w_q, w_k, w_v, w_qkv, w_z, w_sm],
                                          [F32, F32, F32, F32, F32, F32], seq, 3, conv_w, no_bias,
                                          slabbed=(0, 1, 2))
    o_a = _dilated_attention(qa, ka, va, bsz, seq)
    o_b = _gdn(qkv_b, z, sm, _pad_lanes(a_log, B_HEADS), _pad_lanes(dt_bias, B_HEADS),
               head_norm_w.reshape(1, -1), bsz, seq)
    w_o = w_out.astype(BF16)
    return post_w, [o_a, o_b], [w_o[:A_WIDTH], w_o[A_WIDTH:]]


def _odd_mixer(h, pre_w, post_w, w_in, conv_w, conv_b, dt_bias, a_log, d_skip, out_norm_w, w_out, bsz, seq):
    w_z = w_in[:, :C_D_INNER].astype(BF16)
    w_xbc = w_in[:, C_D_INNER:C_D_INNER + C_XBC].astype(BF16)
    w_dt = jnp.pad(w_in[:, C_D_INNER + C_XBC:], ((0, 0), (0, LANES - C_HEADS))).astype(BF16)
    z, xbc, dt = _norm_proj(h, pre_w, [w_z, w_xbc, w_dt], [F32, F32, F32], seq, 1, conv_w, conv_b.reshape(1, -1))
    y = _ssd(xbc, z, dt, _pad_lanes(dt_bias), _pad_lanes(a_log),
             jnp.repeat(d_skip, C_HEAD_DIM).reshape(1, -1), out_norm_w.reshape(1, -1), bsz, seq)
    return post_w, [y], [w_out.astype(BF16)]


def kernel(x, norm_w, ffn_w_gate, ffn_w_up, ffn_w_down, even_w_in, even_conv_w, even_a_log, even_dt_bias,
           even_head_norm_w, even_w_out, odd_w_in, odd_conv_w, odd_conv_b, odd_dt_bias, odd_a_log, odd_d_skip,
           odd_out_norm_w, odd_w_out):
    bsz, seq, d = x.shape
    depth = norm_w.shape[0]
    h = x.reshape(bsz * seq, d)
    wg, wu, wd = ffn_w_gate.astype(BF16), ffn_w_up.astype(BF16), ffn_w_down.astype(BF16)
    for layer in range(depth):
        nw = norm_w[layer][:, None, :]
        i = layer // 2
        h = _ffn(h, nw[0], nw[1], wg[layer, 0], wu[layer, 0], wd[layer, 0])
        if layer % 2 == 0:
            mixer = _even_mixer(h, nw[2], nw[3], even_w_in[i], even_conv_w[i], even_a_log[i], even_dt_bias[i],
                                even_head_norm_w[i], even_w_out[i], bsz, seq)
        else:
            mixer = _odd_mixer(h, nw[2], nw[3], odd_w_in[i], odd_conv_w[i], odd_conv_b[i], odd_dt_bias[i],
                               odd_a_log[i], odd_d_skip[i], odd_out_norm_w[i], odd_w_out[i], bsz, seq)
        h = _ffn(h, nw[4], nw[5], wg[layer, 1], wu[layer, 1], wd[layer, 1], mixer=mixer)
    return h.reshape(bsz, seq, d)
```

```python
import functools
import math

import jax
import jax.numpy as jnp
from jax import lax
from jax.experimental import pallas as pl
from jax.experimental.pallas import tpu as pltpu

F32 = jnp.float32
BF16 = jnp.bfloat16
HIGHEST = lax.Precision.HIGHEST

NORM_EPS = 1e-6
FFN_RES_SCALE = 0.5

A_HEADS = 8
A_HEAD_DIM = 64
A_WIDTH = A_HEADS * A_HEAD_DIM
A_DILATIONS = (16, 4, 1)
A_BLOCK = 128

B_HEADS = 4
B_HEAD_DIM = 128
B_WIDTH = B_HEADS * B_HEAD_DIM
B_CONV = 4

C_D_INNER = 2048
C_HEAD_DIM = 64
C_HEADS = C_D_INNER // C_HEAD_DIM
C_GROUPS = 4
C_STATE = 128
C_CONV = 4
C_XBC = C_D_INNER + 2 * C_GROUPS * C_STATE
C_GROUP_WIDTH = C_D_INNER // C_GROUPS
C_HEADS_PER_GROUP = C_HEADS // C_GROUPS

LANES = 128
CONV_TAIL = 8
CONV_COLS = 512
NEG_BIG = -1e30

A_SLABS = A_WIDTH // LANES
ATTN_TOKENS = A_BLOCK * max(A_DILATIONS)
ATTN_SLABS_PER_STEP = 2
ATTN_BLOCKS_PER_TRIP = 4

ROW_TILE = 512
GDN_CHUNK = 128
GDN_CHUNKS_PER_STEP = 4
SSD_CHUNK = 128
VMEM_LIMIT = 56 * 1024 * 1024


def _cparams(*sem):
    return pltpu.CompilerParams(dimension_semantics=sem, vmem_limit_bytes=VMEM_LIMIT)


def _dot(a, b, precision=None):
    return jnp.dot(a, b, preferred_element_type=F32, precision=precision)


def _dot_nt(a, b, precision=None):
    return lax.dot_general(a, b, (((1,), (1,)), ((), ())), preferred_element_type=F32, precision=precision)


def _dot_tn(a, b, precision=None):
    return lax.dot_general(a, b, (((0,), (0,)), ((), ())), preferred_element_type=F32, precision=precision)


def _rms(x, w):
    return x * lax.rsqrt(jnp.mean(x * x, axis=-1, keepdims=True) + NORM_EPS) * w


def _silu(x):
    return x * jax.nn.sigmoid(x)


def _softplus(x):
    return jnp.maximum(x, 0.0) + jnp.log1p(jnp.exp(-jnp.abs(x)))


def _const_spec(shape):
    return pl.BlockSpec(shape, lambda *_: (0,) * len(shape), pipeline_mode=pl.Buffered(1))


def _mixer_out(x_refs, w_refs):
    m = None
    for x_ref, w_ref in zip(x_refs, w_refs):
        if len(x_ref.shape) == 3:
            x = jnp.concatenate([x_ref[s] for s in range(x_ref.shape[0])], axis=1)
        else:
            x = x_ref[...]
        part = _dot(x, w_ref[...])
        m = part if m is None else m + part
    return m


def _ffn_kernel(h_ref, pre_ref, post_ref, wg_ref, wu_ref, wd_ref, *refs):
    o_ref = refs[-1]
    x = h_ref[...]
    if len(refs) > 1:
        n_in = (len(refs) - 2) // 2
        x = x + _rms(_mixer_out(refs[1:1 + n_in], refs[1 + n_in:1 + 2 * n_in]), refs[0][...])
    xn = _rms(x, pre_ref[...]).astype(BF16)
    gate = _dot(xn, wg_ref[...])
    up = _dot(xn, wu_ref[...])
    act = (_silu(gate) * up).astype(BF16)
    f = _dot(act, wd_ref[...])
    o_ref[...] = x + FFN_RES_SCALE * _rms(f, post_ref[...])


def _ffn(h, pre_w, post_w, wg, wu, wd, mixer=None):
    t, d = h.shape
    ff = wg.shape[1]
    tm = min(ROW_TILE, t)
    row = lambda n: pl.BlockSpec((tm, n), lambda i: (i, 0))
    operands = [h, pre_w, post_w, wg, wu, wd]
    in_specs = [row(d), _const_spec((1, d)), _const_spec((1, d)), _const_spec((d, ff)), _const_spec((d, ff)),
                _const_spec((ff, d))]
    if mixer is not None:
        nw, xs, weights = mixer
        operands += [nw, *xs, *weights]
        in_specs += [_const_spec((1, d))]
        in_specs += [pl.BlockSpec((x.shape[0], tm, LANES), lambda i: (0, i, 0)) if x.ndim == 3 else row(x.shape[1])
                     for x in xs]
        in_specs += [_const_spec(w.shape) for w in weights]
    return pl.pallas_call(
        _ffn_kernel,
        grid=(t // tm,),
        in_specs=in_specs,
        out_specs=row(d),
        out_shape=jax.ShapeDtypeStruct((t, d), F32),
        compiler_params=_cparams("parallel"),
        name="ffn",
    )(*operands)


def _norm_proj_kernel(h_ref, hprev_ref, nw_ref, cw_ref, cb_ref, *refs, conv_index, tiles_per_seq):
    n_out = len(refs) // 2
    nw = nw_ref[...]
    xn = _rms(h_ref[...], nw).astype(BF16)
    x_ext = jnp.concatenate([_rms(hprev_ref[...], nw).astype(BF16), xn], axis=0)
    keep_tail = jnp.where(pl.program_id(0) % tiles_per_seq == 0, 0.0, 1.0)
    k_len = cw_ref.shape[0]
    cw_w, co_ref = refs[conv_index], refs[n_out + conv_index]

    def plain_piece(w_ref, o_ref, c0, width):
        res = _dot(xn, w_ref[:, c0:c0 + width]).astype(o_ref.dtype)
        if len(o_ref.shape) == 3:
            for s in range(width // LANES):
                o_ref[c0 // LANES + s] = res[:, s * LANES:(s + 1) * LANES]
        else:
            o_ref[:, c0:c0 + width] = res

    plain = [(refs[i], refs[n_out + i], c0, min(CONV_COLS, refs[i].shape[1]))
             for i in range(n_out) if i != conv_index for c0 in range(0, refs[i].shape[1], CONV_COLS)]
    conv_chunks = [slice(c0, c0 + CONV_COLS) for c0 in range(0, cw_w.shape[1], CONV_COLS)]
    per_chunk = -(-len(plain) // len(conv_chunks))
    res_next = _dot(x_ext, cw_w[:, conv_chunks[0]])
    for c, cs in enumerate(conv_chunks):
        res = res_next
        if c + 1 < len(conv_chunks):
            res_next = _dot(x_ext, cw_w[:, conv_chunks[c + 1]])
        for piece in plain[c * per_chunk:(c + 1) * per_chunk]:
            plain_piece(*piece)
        res = jnp.concatenate([res[0:CONV_TAIL] * keep_tail, res[CONV_TAIL:]], axis=0)
        y = cb_ref[:, cs] + cw_ref[k_len - 1:k_len, cs] * res[CONV_TAIL:]
        for shift in range(1, k_len):
            y = y + cw_ref[k_len - 1 - shift:k_len - shift, cs] * pltpu.roll(res, shift, axis=0)[CONV_TAIL:]
        co_ref[:, cs] = _silu(y).astype(co_ref.dtype)


def _norm_proj(h, nw, weights, out_dtypes, seq, conv_index, conv_w, conv_b, slabbed=()):
    t, d = h.shape
    tm = min(ROW_TILE, seq)
    row = lambda n: pl.BlockSpec((tm, n), lambda i: (i, 0))
    slab = lambda n: pl.BlockSpec((n // LANES, tm, LANES), lambda i: (0, i, 0))
    tail = pl.BlockSpec((CONV_TAIL, d), lambda i: (jnp.maximum(i * (tm // CONV_TAIL) - 1, 0), 0))
    widths = [w.shape[1] for w in weights]
    conv_width = widths[conv_index]
    return pl.pallas_call(
        functools.partial(_norm_proj_kernel, conv_index=conv_index, tiles_per_seq=seq // tm),
        grid=(t // tm,),
        in_specs=[row(d), tail, _const_spec((1, d)), _const_spec(conv_w.shape), _const_spec((1, conv_width))]
        + [_const_spec(w.shape) for w in weights],
        out_specs=[slab(n) if i in slabbed else row(n) for i, n in enumerate(widths)],
        out_shape=[jax.ShapeDtypeStruct((n // LANES, t, LANES) if i in slabbed else (t, n), dt)
                   for i, (n, dt) in enumerate(zip(widths, out_dtypes))],
        compiler_params=_cparams("parallel"),
        name="norm_proj",
    )(h, h, nw, conv_w, conv_b, *weights)


def _attn_kernel(q_ref, kp_ref, kc_ref, vp_ref, vc_ref, o_ref, acc, m_run, l_run):
    n_slabs = q_ref.shape[0]
    tokens = q_ref.shape[1]
    has_prev_block = pl.program_id(2) > 0
    ri = lax.broadcasted_iota(jnp.int32, (2 * A_BLOCK, A_BLOCK), 0) % A_BLOCK
    ci = lax.broadcasted_iota(jnp.int32, (2 * A_BLOCK, A_BLOCK), 1)
    prev_valid2 = ci >= ri
    cur_valid2 = ci <= ri
    low_half = lax.broadcasted_iota(jnp.int32, (A_BLOCK, LANES), 1) < A_HEAD_DIM
    ones_b = jnp.ones((A_BLOCK, LANES), BF16)

    def blocks(dil, starts, kprev_ref, vprev_ref, prev_starts, prev_bias, first, last):
        rows = lambda s0: pl.ds(s0, A_BLOCK, stride=dil) if dil > 1 else pl.ds(s0, A_BLOCK)
        units = [(st, pst, s) for st, pst in zip(starts, prev_starts) for s in range(n_slabs)]
        qm = []
        for st, _, s in units:
            qs = q_ref[s, rows(st), :]
            qm.append(jnp.concatenate([jnp.where(low_half, qs, 0.0), jnp.where(low_half, 0.0, qs)],
                                      axis=0).astype(BF16))
        kc = [kc_ref[s, rows(st), :].astype(BF16) for st, _, s in units]
        kp = [kprev_ref[s, rows(pst), :].astype(BF16) for _, pst, s in units]
        vc = [jnp.concatenate([vc_ref[s, rows(st), :].astype(BF16), ones_b], axis=1) for st, _, s in units]
        vp = [jnp.concatenate([vprev_ref[s, rows(pst), :].astype(BF16), ones_b], axis=1) for _, pst, s in units]
        sps = [jnp.where(prev_valid2, _dot_nt(q2, k), NEG_BIG) + prev_bias for q2, k in zip(qm, kp)]
        scs = [jnp.where(cur_valid2, _dot_nt(q2, k), NEG_BIG) for q2, k in zip(qm, kc)]
        mbs = [jnp.broadcast_to(jnp.maximum(jnp.max(sp, axis=-1, keepdims=True), jnp.max(sc, axis=-1, keepdims=True)),
                                (2 * A_BLOCK, LANES)) for sp, sc in zip(sps, scs)]
        pps = [jnp.exp(sp - mb).astype(BF16) for sp, mb in zip(sps, mbs)]
        pcs = [jnp.exp(sc - mb).astype(BF16) for sc, mb in zip(scs, mbs)]
        nds = [_dot(pp, v_p) + _dot(pc, v_c) for pp, pc, v_p, v_c in zip(pps, pcs, vp, vc)]
        for u, (st, _, s) in enumerate(units):
            nd_lo, nd_hi = nds[u][:A_BLOCK], nds[u][A_BLOCK:]
            num = jnp.where(low_half, nd_lo[:, :LANES], nd_hi[:, :LANES])
            den = jnp.where(low_half, nd_lo[:, LANES:], nd_hi[:, LANES:])
            m = jnp.where(low_half, mbs[u][:A_BLOCK], mbs[u][A_BLOCK:])
            if not first:
                m_in = m_run[s, rows(st), :]
                m_new = jnp.maximum(m_in, m)
                w_in = jnp.exp(m_in - m_new)
                w_cur = jnp.exp(m - m_new)
                num = acc[s, rows(st), :] * w_in + num * w_cur
                den = l_run[s, rows(st), :] * w_in + den * w_cur
                m = m_new
            if last:
                acc[s, rows(st), :] = num / den
            else:
                acc[s, rows(st), :] = num
                m_run[s, rows(st), :] = m
                l_run[s, rows(st), :] = den

    def for_each_group(count, run):
        rem = count % ATTN_BLOCKS_PER_TRIP
        if rem:
            run(list(range(rem)))

        def trip(t, carry):
            run([rem + t * ATTN_BLOCKS_PER_TRIP + u for u in range(ATTN_BLOCKS_PER_TRIP)])
            return carry

        if count // ATTN_BLOCKS_PER_TRIP:
            lax.fori_loop(0, count // ATTN_BLOCKS_PER_TRIP, trip, 0)

    no_prev_bias = jnp.where(has_prev_block, 0.0, NEG_BIG)
    for p, dil in enumerate(A_DILATIONS):
        first, last = p == 0, p == len(A_DILATIONS) - 1
        sub = A_BLOCK * dil
        n_sub = tokens // sub

        def from_prev_block(rs, dil=dil, sub=sub, first=first, last=last):
            blocks(dil, rs, kp_ref, vp_ref, [tokens - sub + r for r in rs], no_prev_bias, first, last)

        def from_this_block(idx, dil=dil, sub=sub, first=first, last=last):
            starts = [(1 + i // dil) * sub + i % dil for i in idx]
            blocks(dil, starts, kc_ref, vc_ref, [st - sub for st in starts], 0.0, first, last)

        for_each_group(dil, from_prev_block)
        for_each_group((n_sub - 1) * dil, from_this_block)
    o_ref[...] = acc[...].astype(o_ref.dtype)


def _dilated_attention(q, k, v, bsz, seq):
    t = bsz * seq
    nblk = seq // ATTN_TOKENS
    shape = (ATTN_SLABS_PER_STEP, ATTN_TOKENS, LANES)
    cur = pl.BlockSpec(shape, lambda b, g, j: (g, b * nblk + j, 0))
    prev = pl.BlockSpec(shape, lambda b, g, j: (g, b * nblk + jnp.maximum(j - 1, 0), 0))
    return pl.pallas_call(
        _attn_kernel,
        grid=(bsz, A_SLABS // ATTN_SLABS_PER_STEP, nblk),
        in_specs=[cur, prev, cur, prev, cur],
        out_specs=cur,
        out_shape=jax.ShapeDtypeStruct((A_SLABS, t, LANES), BF16),
        scratch_shapes=[pltpu.VMEM(shape, F32)] * 3,
        compiler_params=_cparams("parallel", "parallel", "parallel"),
        name="attn",
    )(q, k, k, v, v)


def _split_bf16(a):
    hi = a.astype(BF16)
    lo = (a - hi.astype(F32)).astype(BF16)
    return hi, lo


def _unit_lower_inverses(a_list, eye, ri, ci):
    n = eye.shape[0]

    def joins(b):
        return ((ri // (2 * b)) == (ci // (2 * b))) & ((ri % (2 * b)) >= b) & ((ci % (2 * b)) < b)

    a_bs = [a.astype(BF16) for a in a_list]
    zero_b = jnp.zeros((n, n), BF16)
    xs = [eye - jnp.where(joins(1), a, 0.0) for a in a_list]
    b = 2
    while b < n:
        mask = joins(b)
        x_bs = [x.astype(BF16) for x in xs]
        ys = [_dot(xb, jnp.where(mask, ab, zero_b)).astype(BF16) for xb, ab in zip(x_bs, a_bs)]
        xs = [x - _dot(y, xb) for x, y, xb in zip(xs, ys, x_bs)]
        b *= 2
    return xs


def _gdn_kernel(qkv_ref, z_ref, sm_ref, alog_ref, dtb_ref, hnw_ref, o_ref, state):
    rows = qkv_ref.shape[0]
    y = qkv_ref[...]

    @pl.when(pl.program_id(1) == 0)
    def _():
        state[...] = jnp.zeros(state.shape, F32)

    sm = sm_ref[...]
    beta = jax.nn.sigmoid(sm)
    g = -jnp.exp(alog_ref[...]) * _softplus(sm + dtb_ref[...])
    ri = lax.broadcasted_iota(jnp.int32, (GDN_CHUNK, GDN_CHUNK), 0)
    ci = lax.broadcasted_iota(jnp.int32, (GDN_CHUNK, GDN_CHUNK), 1)
    tri = ci <= ri
    strict = ci < ri
    eye = jnp.where(ci == ri, 1.0, 0.0)
    ones_tri = jnp.where(tri, 1.0, 0.0)
    hnw = hnw_ref[...]
    n_chunks = rows // GDN_CHUNK
    a_mats, qks, rhss, q_decs, k_decs, keeps = [], [], [], [], [], []
    for c in range(n_chunks):
        rs = slice(c * GDN_CHUNK, (c + 1) * GDN_CHUNK)
        gc = _dot(ones_tri, g[rs], HIGHEST)
        gc_t = gc.T
        g_last = gc[GDN_CHUNK - 1:GDN_CHUNK, :]
        e_gc = jnp.exp(gc)
        e_rem = jnp.exp(g_last - gc)
        e_last = jnp.exp(g_last)
        for hd in range(B_HEADS):
            lane = B_HEADS + hd
            col = lambda base: slice(base + hd * B_HEAD_DIM, base + (hd + 1) * B_HEAD_DIM)
            qh, kh, vh = y[rs, col(0)], y[rs, col(B_WIDTH)], y[rs, col(2 * B_WIDTH)]
            qh = qh * (lax.rsqrt(jnp.sum(qh * qh, axis=-1, keepdims=True) + NORM_EPS) * B_HEAD_DIM ** -0.5)
            kh = kh * lax.rsqrt(jnp.sum(kh * kh, axis=-1, keepdims=True) + NORM_EPS)
            b_col = beta[rs, hd:hd + 1]
            e_col = e_gc[:, lane:lane + 1]
            decay = jnp.exp(jnp.where(tri, gc[:, lane:lane + 1] - gc_t[lane:lane + 1, :], NEG_BIG))
            kb = kh.astype(BF16)
            qk_kk = _dot_nt(jnp.concatenate([qh.astype(BF16), kb], axis=0), kb)
            qks.append(jnp.where(tri, qk_kk[:GDN_CHUNK] * decay, 0.0).astype(BF16))
            a_mats.append(jnp.where(strict, qk_kk[GDN_CHUNK:] * decay, 0.0) * b_col)
            rhss.append(jnp.concatenate([vh * b_col, kh * (b_col * e_col)], axis=1).astype(BF16))
            q_decs.append((qh * e_col).astype(BF16))
            k_decs.append((kh * e_rem[:, lane:lane + 1]).astype(BF16))
            keeps.append(e_last[:, lane:lane + 1])
    t_invs = _unit_lower_inverses(a_mats, eye, ri, ci)
    uws = [_dot(t.astype(BF16), rhs) for t, rhs in zip(t_invs, rhss)]
    for c in range(n_chunks):
        rs = slice(c * GDN_CHUNK, (c + 1) * GDN_CHUNK)
        for hd in range(B_HEADS):
            i = c * B_HEADS + hd
            col = slice(hd * B_HEAD_DIM, (hd + 1) * B_HEAD_DIM)
            uw = uws[i]
            s = state[hd]
            ws_qs = _dot(jnp.concatenate([uw[:, B_HEAD_DIM:].astype(BF16), q_decs[i]], axis=0), s.astype(BF16))
            vb = (uw[:, :B_HEAD_DIM] - ws_qs[:GDN_CHUNK]).astype(BF16)
            o = ws_qs[GDN_CHUNK:] + _dot(qks[i], vb)
            state[hd] = s * keeps[i] + _dot_tn(k_decs[i], vb)
            o = _rms(o, hnw) * _silu(z_ref[rs, col])
            o_ref[rs, col] = o.astype(o_ref.dtype)


def _gdn(qkv, z, sm, alog_row, dtb_row, hnw, bsz, seq):
    rows = GDN_CHUNKS_PER_STEP * GDN_CHUNK
    steps = seq // rows
    width = qkv.shape[1]
    row = lambda n: pl.BlockSpec((rows, n), lambda b, s: (b * steps + s, 0))
    return pl.pallas_call(
        _gdn_kernel,
        grid=(bsz, steps),
        in_specs=[row(width), row(B_WIDTH), row(LANES), _const_spec((1, LANES)), _const_spec((1, LANES)),
                  _const_spec((1, B_HEAD_DIM))],
        out_specs=row(B_WIDTH),
        out_shape=jax.ShapeDtypeStruct((bsz * seq, B_WIDTH), BF16),
        scratch_shapes=[pltpu.VMEM((B_HEADS, B_HEAD_DIM, B_HEAD_DIM), F32)],
        compiler_params=_cparams("parallel", "arbitrary"),
        name="gdn",
    )(qkv, z, sm, alog_row, dtb_row, hnw)


def _ssd_kernel(xbc_ref, z_ref, dt_ref, dtb_ref, alog_ref, dsk_ref, onw_ref, expand_ref, o_ref, state):
    rows = xbc_ref.shape[0]
    y = xbc_ref[...]

    @pl.when(pl.program_id(1) == 0)
    def _():
        state[...] = jnp.zeros(state.shape, F32)

    dt = _softplus(dt_ref[...] + dtb_ref[...])
    a = -jnp.exp(alog_ref[...]) * dt
    ri = lax.broadcasted_iota(jnp.int32, (rows, rows), 0)
    ci = lax.broadcasted_iota(jnp.int32, (rows, rows), 1)
    tri = ci <= ri
    acum = _dot(jnp.where(tri, 1.0, 0.0), a, HIGHEST)
    acum_t = acum.T
    a_last = acum[rows - 1:rows, :]
    expand = expand_ref[...]
    spread = lambda cols: _dot(jnp.concatenate(_split_bf16(cols), axis=1), expand)
    dt_x = spread(dt)
    e_acum_x = spread(jnp.exp(acum))
    e_rem_x = spread(jnp.exp(a_last - acum))
    e_last_x = spread(jnp.broadcast_to(jnp.exp(a_last), (CONV_TAIL, LANES)))[0:1, :]
    lane = lax.broadcasted_iota(jnp.int32, (rows, LANES), 1)
    low_half = lane < C_HEAD_DIM
    zero_b = jnp.zeros((rows, LANES), BF16)
    b_gs = [y[:, C_D_INNER + g * C_STATE:C_D_INNER + (g + 1) * C_STATE].astype(BF16) for g in range(C_GROUPS)]
    c_gs = [y[:, C_D_INNER + (C_GROUPS + g) * C_STATE:C_D_INNER + (C_GROUPS + g + 1) * C_STATE].astype(BF16)
            for g in range(C_GROUPS)]
    cbs = [_dot_nt(c_g, b_g) for c_g, b_g in zip(c_gs, b_gs)]
    sts = [state[g] for g in range(C_GROUPS)]
    y_offs = [_dot(c_g, st.astype(BF16)) for c_g, st in zip(c_gs, sts)]
    segs = []
    for hd in range(C_HEADS):
        seg = jnp.exp(jnp.where(tri, acum[:, hd:hd + 1] - acum_t[hd:hd + 1, :], NEG_BIG))
        segs.append((cbs[hd // C_HEADS_PER_GROUP] * seg).astype(BF16))
    pairs_per_group = C_HEADS_PER_GROUP // 2
    for g in range(C_GROUPS):
        outs, xdecs = [], []
        sumsq = jnp.zeros((rows, 1), F32)
        for q in range(pairs_per_group):
            pair = g * pairs_per_group + q
            blk = slice(pair * LANES, (pair + 1) * LANES)
            xs = y[:, blk]
            xdt = xs * dt_x[:, blk]
            xdt_b = xdt.astype(BF16)
            rhs = jnp.concatenate([jnp.where(low_half, xdt_b, zero_b), jnp.where(low_half, zero_b, xdt_b)], axis=0)
            y_diag = _dot(jnp.concatenate([segs[2 * pair], segs[2 * pair + 1]], axis=1), rhs)
            y_off = y_offs[g][:, q * LANES:(q + 1) * LANES] * e_acum_x[:, blk]
            out = (y_diag + y_off + dsk_ref[:, blk] * xs) * _silu(z_ref[:, blk])
            sumsq = sumsq + jnp.sum(out * out, axis=-1, keepdims=True)
            outs.append(out)
            xdecs.append((xdt * e_rem_x[:, blk]).astype(BF16))
        gsl = slice(g * C_GROUP_WIDTH, (g + 1) * C_GROUP_WIDTH)
        state[g] = sts[g] * e_last_x[:, gsl] + _dot_tn(b_gs[g], jnp.concatenate(xdecs, axis=1))
        scale = lax.rsqrt(sumsq * (1.0 / C_GROUP_WIDTH) + NORM_EPS)
        for q, out in enumerate(outs):
            blk = slice((g * pairs_per_group + q) * LANES, (g * pairs_per_group + q + 1) * LANES)
            o_ref[:, blk] = (out * scale * onw_ref[:, blk]).astype(o_ref.dtype)


def _ssd(xbc, z, dt, dtb_row, alog_row, dsk_row, onw_row, bsz, seq):
    rows = SSD_CHUNK
    steps = seq // rows
    row = lambda n: pl.BlockSpec((rows, n), lambda b, s: (b * steps + s, 0))
    head_of_channel = jnp.arange(C_D_INNER, dtype=jnp.int32) // C_HEAD_DIM
    expand = (jnp.arange(2 * LANES, dtype=jnp.int32)[:, None] % LANES == head_of_channel[None, :]).astype(BF16)
    return pl.pallas_call(
        _ssd_kernel,
        grid=(bsz, steps),
        in_specs=[row(C_XBC), row(C_D_INNER), row(LANES),
                  _const_spec((1, LANES)), _const_spec((1, LANES)), _const_spec((1, C_D_INNER)),
                  _const_spec((1, C_D_INNER)), _const_spec((2 * LANES, C_D_INNER))],
        out_specs=row(C_D_INNER),
        out_shape=jax.ShapeDtypeStruct((bsz * seq, C_D_INNER), BF16),
        scratch_shapes=[pltpu.VMEM((C_GROUPS, C_STATE, C_GROUP_WIDTH), F32)],
        compiler_params=_cparams("parallel", "arbitrary"),
        name="ssd",
    )(xbc, z, dt, dtb_row, alog_row, dsk_row, onw_row, expand)


def _pad_lanes(x, offset=0):
    x = x.reshape(-1, x.shape[-1])
    return jnp.pad(x, ((0, 0), (offset, LANES - offset - x.shape[-1])))


def _even_mixer(h, pre_w, post_w, w_in, conv_w, a_log, dt_bias, head_norm_w, w_out, bsz, seq):
    cuts = [A_WIDTH, 2 * A_WIDTH, 3 * A_WIDTH, 3 * A_WIDTH + 3 * B_WIDTH, 3 * A_WIDTH + 4 * B_WIDTH]
    w_in = w_in.astype(BF16)
    w_q = w_in[:, :cuts[0]] * A_HEAD_DIM ** -0.5
    w_k = w_in[:, cuts[0]:cuts[1]]
    w_v = w_in[:, cuts[1]:cuts[2]]
    w_qkv = w_in[:, cuts[2]:cuts[3]]
    w_z = w_in[:, cuts[3]:cuts[4]]
    w_sm = jnp.pad(w_in[:, cuts[4]:], ((0, 0), (0, LANES - 2 * B_HEADS)))
    no_bias = jnp.zeros((1, 3 * B_WIDTH), F32)
    qa, ka, va, qkv_b, z, sm = _norm_proj(h, pre_w, [w_q, w_k, w_v, w_qkv, w_z, w_sm],
                                          [F32, F32, F32, F32, F32, F32], seq, 3, conv_w, no_bias,
                                          slabbed=(0, 1, 2))
    o_a = _dilated_attention(qa, ka, va, bsz, seq)
    o_b = _gdn(qkv_b, z, sm, _pad_lanes(a_log, B_HEADS), _pad_lanes(dt_bias, B_HEADS),
               head_norm_w.reshape(1, -1), bsz, seq)
    w_o = w_out.astype(BF16)
    return post_w, [o_a, o_b], [w_o[:A_WIDTH], w_o[A_WIDTH:]]


def _odd_mixer(h, pre_w, post_w, w_in, conv_w, conv_b, dt_bias, a_log, d_skip, out_norm_w, w_out, bsz, seq):
    w_in = w_in.astype(BF16)
    w_z = w_in[:, :C_D_INNER]
    w_xbc = w_in[:, C_D_INNER:C_D_INNER + C_XBC]
    w_dt = jnp.pad(w_in[:, C_D_INNER + C_XBC:], ((0, 0), (0, LANES - C_HEADS)))
    z, xbc, dt = _norm_proj(h, pre_w, [w_z, w_xbc, w_dt], [F32, F32, F32], seq, 1, conv_w, conv_b.reshape(1, -1))
    y = _ssd(xbc, z, dt, _pad_lanes(dt_bias), _pad_lanes(a_log),
             jnp.repeat(d_skip, C_HEAD_DIM).reshape(1, -1), out_norm_w.reshape(1, -1), bsz, seq)
    return post_w, [y], [w_out.astype(BF16)]


def kernel(x, norm_w, ffn_w_gate, ffn_w_up, ffn_w_down, even_w_in, even_conv_w, even_a_log, even_dt_bias,
           even_head_norm_w, even_w_out, odd_w_in, odd_conv_w, odd_conv_b, odd_dt_bias, odd_a_log, odd_d_skip,
           odd_out_norm_w, odd_w_out):
    bsz, seq, d = x.shape
    depth = norm_w.shape[0]
    h = x.reshape(bsz * seq, d)
    wg, wu, wd = ffn_w_gate.astype(BF16), ffn_w_up.astype(BF16), ffn_w_down.astype(BF16)
    for layer in range(depth):
        nw = norm_w[layer][:, None, :]
        i = layer // 2
        h = _ffn(h, nw[0], nw[1], wg[layer, 0], wu[layer, 0], wd[layer, 0])
        if layer % 2 == 0:
            mixer = _even_mixer(h, nw[2], nw[3], even_w_in[i], even_conv_w[i], even_a_log[i], even_dt_bias[i],
                                even_head_norm_w[i], even_w_out[i], bsz, seq)
        else:
            mixer = _odd_mixer(h, nw[2], nw[3], odd_w_in[i], odd_conv_w[i], odd_conv_b[i], odd_dt_bias[i],
                               odd_a_log[i], odd_d_skip[i], odd_out_norm_w[i], odd_w_out[i], bsz, seq)
        h = _ffn(h, nw[4], nw[5], wg[layer, 1], wu[layer, 1], wd[layer, 1], mixer=mixer)
    return h.reshape(bsz, seq, d)
```

```python
import functools

import jax
import jax.numpy as jnp
from jax import lax
from jax.experimental import pallas as pl
from jax.experimental.pallas import tpu as pltpu

F32 = jnp.float32
BF16 = jnp.bfloat16
HIGHEST = lax.Precision.HIGHEST

NORM_EPS = 1e-6
FFN_RES_SCALE = 0.5

A_HEADS = 8
A_HEAD_DIM = 64
A_WIDTH = A_HEADS * A_HEAD_DIM
A_DILATIONS = (16, 4, 1)
A_BLOCK = 128

B_HEADS = 4
B_HEAD_DIM = 128
B_WIDTH = B_HEADS * B_HEAD_DIM

C_D_INNER = 2048
C_HEAD_DIM = 64
C_HEADS = C_D_INNER // C_HEAD_DIM
C_GROUPS = 4
C_STATE = 128
C_XBC = C_D_INNER + 2 * C_GROUPS * C_STATE
C_GROUP_WIDTH = C_D_INNER // C_GROUPS
C_HEADS_PER_GROUP = C_HEADS // C_GROUPS

LANES = 128
CONV_TAIL = 8
CONV_COLS = 512
NEG_BIG = -1e30

A_SLABS = A_WIDTH // LANES
ATTN_TOKENS = A_BLOCK * max(A_DILATIONS)
ATTN_SLABS_PER_STEP = 2
ATTN_BLOCKS_PER_TRIP = 4

ROW_TILE = 512
GDN_CHUNK = 128
GDN_CHUNKS_PER_STEP = 4
SSD_CHUNK = 128
VMEM_LIMIT = 56 * 1024 * 1024


def _cparams(*sem):
    return pltpu.CompilerParams(dimension_semantics=sem, vmem_limit_bytes=VMEM_LIMIT)


def _dot(a, b, precision=None):
    return jnp.dot(a, b, preferred_element_type=F32, precision=precision)


def _dot_nt(a, b, precision=None):
    return lax.dot_general(a, b, (((1,), (1,)), ((), ())), preferred_element_type=F32, precision=precision)


def _dot_tn(a, b, precision=None):
    return lax.dot_general(a, b, (((0,), (0,)), ((), ())), preferred_element_type=F32, precision=precision)


def _rms(x, w):
    return x * lax.rsqrt(jnp.mean(x * x, axis=-1, keepdims=True) + NORM_EPS) * w


def _silu(x):
    return x * jax.nn.sigmoid(x)


def _softplus(x):
    return jnp.maximum(x, 0.0) + jnp.log1p(jnp.exp(-jnp.abs(x)))


def _const_spec(shape):
    return pl.BlockSpec(shape, lambda *_: (0,) * len(shape), pipeline_mode=pl.Buffered(1))


def _mixer_out(x_refs, w_refs):
    m = None
    for x_ref, w_ref in zip(x_refs, w_refs):
        if len(x_ref.shape) == 3:
            x = jnp.concatenate([x_ref[s] for s in range(x_ref.shape[0])], axis=1)
        else:
            x = x_ref[...]
        part = _dot(x, w_ref[...])
        m = part if m is None else m + part
    return m


def _ffn_kernel(h_ref, pre_ref, post_ref, wg_ref, wu_ref, wd_ref, *refs):
    o_ref = refs[-1]
    x = h_ref[...]
    if len(refs) > 1:
        n_in = (len(refs) - 2) // 2
        x = x + _rms(_mixer_out(refs[1:1 + n_in], refs[1 + n_in:1 + 2 * n_in]), refs[0][...])
    xn = _rms(x, pre_ref[...]).astype(BF16)
    gate = _dot(xn, wg_ref[...])
    up = _dot(xn, wu_ref[...])
    act = (_silu(gate) * up).astype(BF16)
    f = _dot(act, wd_ref[...])
    o_ref[...] = x + FFN_RES_SCALE * _rms(f, post_ref[...])


def _ffn(h, pre_w, post_w, wg, wu, wd, mixer=None):
    t, d = h.shape
    ff = wg.shape[1]
    tm = min(ROW_TILE, t)
    row = lambda n: pl.BlockSpec((tm, n), lambda i: (i, 0))
    operands = [h, pre_w, post_w, wg, wu, wd]
    in_specs = [row(d), _const_spec((1, d)), _const_spec((1, d)), _const_spec((d, ff)), _const_spec((d, ff)),
                _const_spec((ff, d))]
    if mixer is not None:
        nw, xs, weights = mixer
        operands += [nw, *xs, *weights]
        in_specs += [_const_spec((1, d))]
        in_specs += [pl.BlockSpec((x.shape[0], tm, LANES), lambda i: (0, i, 0)) if x.ndim == 3 else row(x.shape[1])
                     for x in xs]
        in_specs += [_const_spec(w.shape) for w in weights]
    return pl.pallas_call(
        _ffn_kernel,
        grid=(t // tm,),
        in_specs=in_specs,
        out_specs=row(d),
        out_shape=jax.ShapeDtypeStruct((t, d), F32),
        compiler_params=_cparams("parallel"),
        name="ffn",
    )(*operands)


def _norm_proj_kernel(h_ref, hprev_ref, nw_ref, cw_ref, cb_ref, *refs, conv_index, tiles_per_seq):
    n_out = len(refs) // 2
    nw = nw_ref[...]
    xn = _rms(h_ref[...], nw).astype(BF16)
    x_ext = jnp.concatenate([_rms(hprev_ref[...], nw).astype(BF16), xn], axis=0)
    keep_tail = jnp.where(pl.program_id(0) % tiles_per_seq == 0, 0.0, 1.0)
    k_len = cw_ref.shape[0]
    cw_w, co_ref = refs[conv_index], refs[n_out + conv_index]

    def plain_piece(w_ref, o_ref, c0, width):
        res = _dot(xn, w_ref[:, c0:c0 + width]).astype(o_ref.dtype)
        if len(o_ref.shape) == 3:
            for s in range(width // LANES):
                o_ref[c0 // LANES + s] = res[:, s * LANES:(s + 1) * LANES]
        else:
            o_ref[:, c0:c0 + width] = res

    plain = [(refs[i], refs[n_out + i], c0, min(CONV_COLS, refs[i].shape[1]))
             for i in range(n_out) if i != conv_index for c0 in range(0, refs[i].shape[1], CONV_COLS)]
    conv_chunks = [slice(c0, c0 + CONV_COLS) for c0 in range(0, cw_w.shape[1], CONV_COLS)]
    per_chunk = -(-len(plain) // len(conv_chunks))
    res_next = _dot(x_ext, cw_w[:, conv_chunks[0]])
    for c, cs in enumerate(conv_chunks):
        res = res_next
        if c + 1 < len(conv_chunks):
            res_next = _dot(x_ext, cw_w[:, conv_chunks[c + 1]])
        for piece in plain[c * per_chunk:(c + 1) * per_chunk]:
            plain_piece(*piece)
        res = jnp.concatenate([res[0:CONV_TAIL] * keep_tail, res[CONV_TAIL:]], axis=0)
        y = cb_ref[:, cs] + cw_ref[k_len - 1:k_len, cs] * res[CONV_TAIL:]
        for shift in range(1, k_len):
            y = y + cw_ref[k_len - 1 - shift:k_len - shift, cs] * pltpu.roll(res, shift, axis=0)[CONV_TAIL:]
        co_ref[:, cs] = _silu(y).astype(co_ref.dtype)


def _norm_proj(h, nw, weights, out_dtypes, seq, conv_index, conv_w, conv_b, slabbed=()):
    t, d = h.shape
    tm = min(ROW_TILE, seq)
    row = lambda n: pl.BlockSpec((tm, n), lambda i: (i, 0))
    slab = lambda n: pl.BlockSpec((n // LANES, tm, LANES), lambda i: (0, i, 0))
    tail = pl.BlockSpec((CONV_TAIL, d), lambda i: (jnp.maximum(i * (tm // CONV_TAIL) - 1, 0), 0))
    widths = [w.shape[1] for w in weights]
    conv_width = widths[conv_index]
    return pl.pallas_call(
        functools.partial(_norm_proj_kernel, conv_index=conv_index, tiles_per_seq=seq // tm),
        grid=(t // tm,),
        in_specs=[row(d), tail, _const_spec((1, d)), _const_spec(conv_w.shape), _const_spec((1, conv_width))]
        + [_const_spec(w.shape) for w in weights],
        out_specs=[slab(n) if i in slabbed else row(n) for i, n in enumerate(widths)],
        out_shape=[jax.ShapeDtypeStruct((n // LANES, t, LANES) if i in slabbed else (t, n), dt)
                   for i, (n, dt) in enumerate(zip(widths, out_dtypes))],
        compiler_params=_cparams("parallel"),
        name="norm_proj",
    )(h, h, nw, conv_w, conv_b, *weights)


def _attn_kernel(q_ref, kp_ref, kc_ref, vp_ref, vc_ref, o_ref, acc, m_run, l_run):
    n_slabs = q_ref.shape[0]
    tokens = q_ref.shape[1]
    has_prev_block = pl.program_id(2) > 0
    ri = lax.broadcasted_iota(jnp.int32, (2 * A_BLOCK, A_BLOCK), 0) % A_BLOCK
    ci = lax.broadcasted_iota(jnp.int32, (2 * A_BLOCK, A_BLOCK), 1)
    prev_valid2 = ci >= ri
    cur_valid2 = ci <= ri
    low_half = lax.broadcasted_iota(jnp.int32, (A_BLOCK, LANES), 1) < A_HEAD_DIM
    ones_b = jnp.ones((A_BLOCK, LANES), BF16)

    def blocks(dil, starts, kprev_ref, vprev_ref, prev_starts, prev_bias, first, last):
        rows = lambda s0: pl.ds(s0, A_BLOCK, stride=dil) if dil > 1 else pl.ds(s0, A_BLOCK)
        units = [(st, pst, s) for st, pst in zip(starts, prev_starts) for s in range(n_slabs)]
        qm = []
        for st, _, s in units:
            qs = q_ref[s, rows(st), :]
            qm.append(jnp.concatenate([jnp.where(low_half, qs, 0.0), jnp.where(low_half, 0.0, qs)],
                                      axis=0).astype(BF16))
        kc = [kc_ref[s, rows(st), :].astype(BF16) for st, _, s in units]
        kp = [kprev_ref[s, rows(pst), :].astype(BF16) for _, pst, s in units]
        vc = [jnp.concatenate([vc_ref[s, rows(st), :].astype(BF16), ones_b], axis=1) for st, _, s in units]
        vp = [jnp.concatenate([vprev_ref[s, rows(pst), :].astype(BF16), ones_b], axis=1) for _, pst, s in units]
        sps = [jnp.where(prev_valid2, _dot_nt(q2, k), NEG_BIG) + prev_bias for q2, k in zip(qm, kp)]
        scs = [jnp.where(cur_valid2, _dot_nt(q2, k), NEG_BIG) for q2, k in zip(qm, kc)]
        mbs = [jnp.broadcast_to(jnp.maximum(jnp.max(sp, axis=-1, keepdims=True), jnp.max(sc, axis=-1, keepdims=True)),
                                (2 * A_BLOCK, LANES)) for sp, sc in zip(sps, scs)]
        pps = [jnp.exp(sp - mb).astype(BF16) for sp, mb in zip(sps, mbs)]
        pcs = [jnp.exp(sc - mb).astype(BF16) for sc, mb in zip(scs, mbs)]
        nds = [_dot(pp, v_p) + _dot(pc, v_c) for pp, pc, v_p, v_c in zip(pps, pcs, vp, vc)]
        for u, (st, _, s) in enumerate(units):
            nd_lo, nd_hi = nds[u][:A_BLOCK], nds[u][A_BLOCK:]
            num = jnp.where(low_half, nd_lo[:, :LANES], nd_hi[:, :LANES])
            den = jnp.where(low_half, nd_lo[:, LANES:], nd_hi[:, LANES:])
            m = jnp.where(low_half, mbs[u][:A_BLOCK], mbs[u][A_BLOCK:])
            if not first:
                m_in = m_run[s, rows(st), :]
                m_new = jnp.maximum(m_in, m)
                w_in = jnp.exp(m_in - m_new)
                w_cur = jnp.exp(m - m_new)
                num = acc[s, rows(st), :] * w_in + num * w_cur
                den = l_run[s, rows(st), :] * w_in + den * w_cur
                m = m_new
            if last:
                acc[s, rows(st), :] = num / den
            else:
                acc[s, rows(st), :] = num
                m_run[s, rows(st), :] = m
                l_run[s, rows(st), :] = den

    def for_each_group(count, run):
        rem = count % ATTN_BLOCKS_PER_TRIP
        if rem:
            run(list(range(rem)))

        def trip(t, carry):
            run([rem + t * ATTN_BLOCKS_PER_TRIP + u for u in range(ATTN_BLOCKS_PER_TRIP)])
            return carry

        if count // ATTN_BLOCKS_PER_TRIP:
            lax.fori_loop(0, count // ATTN_BLOCKS_PER_TRIP, trip, 0)

    no_prev_bias = jnp.where(has_prev_block, 0.0, NEG_BIG)
    for p, dil in enumerate(A_DILATIONS):
        first, last = p == 0, p == len(A_DILATIONS) - 1
        sub = A_BLOCK * dil
        n_sub = tokens // sub

        def from_prev_block(rs, dil=dil, sub=sub, first=first, last=last):
            blocks(dil, rs, kp_ref, vp_ref, [tokens - sub + r for r in rs], no_prev_bias, first, last)

        def from_this_block(idx, dil=dil, sub=sub, first=first, last=last):
            starts = [(1 + i // dil) * sub + i % dil for i in idx]
            blocks(dil, starts, kc_ref, vc_ref, [st - sub for st in starts], 0.0, first, last)

        for_each_group(dil, from_prev_block)
        for_each_group((n_sub - 1) * dil, from_this_block)
    o_ref[...] = acc[...].astype(o_ref.dtype)


def _dilated_attention(q, k, v, bsz, seq):
    t = bsz * seq
    nblk = seq // ATTN_TOKENS
    shape = (ATTN_SLABS_PER_STEP, ATTN_TOKENS, LANES)
    cur = pl.BlockSpec(shape, lambda b, g, j: (g, b * nblk + j, 0))
    prev = pl.BlockSpec(shape, lambda b, g, j: (g, b * nblk + jnp.maximum(j - 1, 0), 0))
    return pl.pallas_call(
        _attn_kernel,
        grid=(bsz, A_SLABS // ATTN_SLABS_PER_STEP, nblk),
        in_specs=[cur, prev, cur, prev, cur],
        out_specs=cur,
        out_shape=jax.ShapeDtypeStruct((A_SLABS, t, LANES), BF16),
        scratch_shapes=[pltpu.VMEM(shape, F32)] * 3,
        compiler_params=_cparams("parallel", "parallel", "parallel"),
        name="attn",
    )(q, k, k, v, v)


def _split_bf16(a):
    hi = a.astype(BF16)
    lo = (a - hi.astype(F32)).astype(BF16)
    return hi, lo


def _unit_lower_inverses(a_list, eye, ri, ci):
    n = eye.shape[0]

    def joins(b):
        return ((ri // (2 * b)) == (ci // (2 * b))) & ((ri % (2 * b)) >= b) & ((ci % (2 * b)) < b)

    a_bs = [a.astype(BF16) for a in a_list]
    zero_b = jnp.zeros((n, n), BF16)
    xs = [eye - jnp.where(joins(1), a, 0.0) for a in a_list]
    b = 2
    while b < n:
        mask = joins(b)
        x_bs = [x.astype(BF16) for x in xs]
        ys = [_dot(xb, jnp.where(mask, ab, zero_b)).astype(BF16) for xb, ab in zip(x_bs, a_bs)]
        xs = [x - _dot(y, xb) for x, y, xb in zip(xs, ys, x_bs)]
        b *= 2
    return xs


def _gdn_kernel(qkv_ref, z_ref, sm_ref, alog_ref, dtb_ref, hnw_ref, o_ref, state):
    rows = qkv_ref.shape[0]
    y = qkv_ref[...]

    @pl.when(pl.program_id(1) == 0)
    def _():
        state[...] = jnp.zeros(state.shape, F32)

    sm = sm_ref[...]
    beta = jax.nn.sigmoid(sm)
    g = -jnp.exp(alog_ref[...]) * _softplus(sm + dtb_ref[...])
    ri = lax.broadcasted_iota(jnp.int32, (GDN_CHUNK, GDN_CHUNK), 0)
    ci = lax.broadcasted_iota(jnp.int32, (GDN_CHUNK, GDN_CHUNK), 1)
    tri = ci <= ri
    strict = ci < ri
    eye = jnp.where(ci == ri, 1.0, 0.0)
    ones_tri = jnp.where(tri, 1.0, 0.0)
    hnw = hnw_ref[...]
    n_chunks = rows // GDN_CHUNK
    a_mats, qks, rhss, q_decs, k_decs, keeps = [], [], [], [], [], []
    for c in range(n_chunks):
        rs = slice(c * GDN_CHUNK, (c + 1) * GDN_CHUNK)
        gc = _dot(ones_tri, g[rs], HIGHEST)
        gc_t = gc.T
        g_last = gc[GDN_CHUNK - 1:GDN_CHUNK, :]
        e_gc = jnp.exp(gc)
        e_rem = jnp.exp(g_last - gc)
        e_last = jnp.exp(g_last)
        for hd in range(B_HEADS):
            lane = B_HEADS + hd
            col = lambda base: slice(base + hd * B_HEAD_DIM, base + (hd + 1) * B_HEAD_DIM)
            qh, kh, vh = y[rs, col(0)], y[rs, col(B_WIDTH)], y[rs, col(2 * B_WIDTH)]
            qh = qh * (lax.rsqrt(jnp.sum(qh * qh, axis=-1, keepdims=True) + NORM_EPS) * B_HEAD_DIM ** -0.5)
            kh = kh * lax.rsqrt(jnp.sum(kh * kh, axis=-1, keepdims=True) + NORM_EPS)
            b_col = beta[rs, hd:hd + 1]
            e_col = e_gc[:, lane:lane + 1]
            decay = jnp.exp(jnp.where(tri, gc[:, lane:lane + 1] - gc_t[lane:lane + 1, :], NEG_BIG))
            kb = kh.astype(BF16)
            qk_kk = _dot_nt(jnp.concatenate([qh.astype(BF16), kb], axis=0), kb)
            qks.append(jnp.where(tri, qk_kk[:GDN_CHUNK] * decay, 0.0).astype(BF16))
            a_mats.append(jnp.where(strict, qk_kk[GDN_CHUNK:] * decay, 0.0) * b_col)
            rhss.append(jnp.concatenate([vh * b_col, kh * (b_col * e_col)], axis=1).astype(BF16))
            q_decs.append((qh * e_col).astype(BF16))
            k_decs.append((kh * e_rem[:, lane:lane + 1]).astype(BF16))
            keeps.append(e_last[:, lane:lane + 1])
    t_invs = _unit_lower_inverses(a_mats, eye, ri, ci)
    uws = [_dot(t.astype(BF16), rhs) for t, rhs in zip(t_invs, rhss)]
    for c in range(n_chunks):
        rs = slice(c * GDN_CHUNK, (c + 1) * GDN_CHUNK)
        for hd in range(B_HEADS):
            i = c * B_HEADS + hd
            col = slice(hd * B_HEAD_DIM, (hd + 1) * B_HEAD_DIM)
            uw = uws[i]
            s = state[hd]
            ws_qs = _dot(jnp.concatenate([uw[:, B_HEAD_DIM:].astype(BF16), q_decs[i]], axis=0), s.astype(BF16))
            vb = (uw[:, :B_HEAD_DIM] - ws_qs[:GDN_CHUNK]).astype(BF16)
            o = ws_qs[GDN_CHUNK:] + _dot(qks[i], vb)
            state[hd] = s * keeps[i] + _dot_tn(k_decs[i], vb)
            o = _rms(o, hnw) * _silu(z_ref[rs, col])
            o_ref[rs, col] = o.astype(o_ref.dtype)


def _gdn(qkv, z, sm, alog_row, dtb_row, hnw, bsz, seq):
    rows = GDN_CHUNKS_PER_STEP * GDN_CHUNK
    steps = seq // rows
    width = qkv.shape[1]
    row = lambda n: pl.BlockSpec((rows, n), lambda b, s: (b * steps + s, 0))
    return pl.pallas_call(
        _gdn_kernel,
        grid=(bsz, steps),
        in_specs=[row(width), row(B_WIDTH), row(LANES), _const_spec((1, LANES)), _const_spec((1, LANES)),
                  _const_spec((1, B_HEAD_DIM))],
        out_specs=row(B_WIDTH),
        out_shape=jax.ShapeDtypeStruct((bsz * seq, B_WIDTH), BF16),
        scratch_shapes=[pltpu.VMEM((B_HEADS, B_HEAD_DIM, B_HEAD_DIM), F32)],
        compiler_params=_cparams("parallel", "arbitrary"),
        name="gdn",
    )(qkv, z, sm, alog_row, dtb_row, hnw)


def _ssd_kernel(xbc_ref, z_ref, dt_ref, dtb_ref, alog_ref, dsk_ref, onw_ref, expand_ref, o_ref, state):
    rows = xbc_ref.shape[0]
    y = xbc_ref[...]

    @pl.when(pl.program_id(1) == 0)
    def _():
        state[...] = jnp.zeros(state.shape, F32)

    dt = _softplus(dt_ref[...] + dtb_ref[...])
    a = -jnp.exp(alog_ref[...]) * dt
    ri = lax.broadcasted_iota(jnp.int32, (rows, rows), 0)
    ci = lax.broadcasted_iota(jnp.int32, (rows, rows), 1)
    tri = ci <= ri
    acum = _dot(jnp.where(tri, 1.0, 0.0), a, HIGHEST)
    acum_t = acum.T
    a_last = acum[rows - 1:rows, :]
    expand = expand_ref[...]
    spread = lambda cols: _dot(jnp.concatenate(_split_bf16(cols), axis=1), expand)
    dt_x = spread(dt)
    e_acum_x = spread(jnp.exp(acum))
    e_rem_x = spread(jnp.exp(a_last - acum))
    e_last_x = spread(jnp.broadcast_to(jnp.exp(a_last), (CONV_TAIL, LANES)))[0:1, :]
    lane = lax.broadcasted_iota(jnp.int32, (rows, LANES), 1)
    low_half = lane < C_HEAD_DIM
    zero_b = jnp.zeros((rows, LANES), BF16)
    b_gs = [y[:, C_D_INNER + g * C_STATE:C_D_INNER + (g + 1) * C_STATE].astype(BF16) for g in range(C_GROUPS)]
    c_gs = [y[:, C_D_INNER + (C_GROUPS + g) * C_STATE:C_D_INNER + (C_GROUPS + g + 1) * C_STATE].astype(BF16)
            for g in range(C_GROUPS)]
    cbs = [_dot_nt(c_g, b_g) for c_g, b_g in zip(c_gs, b_gs)]
    sts = [state[g] for g in range(C_GROUPS)]
    y_offs = [_dot(c_g, st.astype(BF16)) for c_g, st in zip(c_gs, sts)]
    segs = []
    for hd in range(C_HEADS):
        seg = jnp.exp(jnp.where(tri, acum[:, hd:hd + 1] - acum_t[hd:hd + 1, :], NEG_BIG))
        segs.append((cbs[hd // C_HEADS_PER_GROUP] * seg).astype(BF16))
    pairs_per_group = C_HEADS_PER_GROUP // 2
    for g in range(C_GROUPS):
        outs, xdecs = [], []
        sumsq = jnp.zeros((rows, 1), F32)
        for q in range(pairs_per_group):
            pair = g * pairs_per_group + q
            blk = slice(pair * LANES, (pair + 1) * LANES)
            xs = y[:, blk]
            xdt = xs * dt_x[:, blk]
            xdt_b = xdt.astype(BF16)
            rhs = jnp.concatenate([jnp.where(low_half, xdt_b, zero_b), jnp.where(low_half, zero_b, xdt_b)], axis=0)
            y_diag = _dot(jnp.concatenate([segs[2 * pair], segs[2 * pair + 1]], axis=1), rhs)
            y_off = y_offs[g][:, q * LANES:(q + 1) * LANES] * e_acum_x[:, blk]
            out = (y_diag + y_off + dsk_ref[:, blk] * xs) * _silu(z_ref[:, blk])
            sumsq = sumsq + jnp.sum(out * out, axis=-1, keepdims=True)
            outs.append(out)
            xdecs.append((xdt * e_rem_x[:, blk]).astype(BF16))
        gsl = slice(g * C_GROUP_WIDTH, (g + 1) * C_GROUP_WIDTH)
        state[g] = sts[g] * e_last_x[:, gsl] + _dot_tn(b_gs[g], jnp.concatenate(xdecs, axis=1))
        scale = lax.rsqrt(sumsq * (1.0 / C_GROUP_WIDTH) + NORM_EPS)
        for q, out in enumerate(outs):
            blk = slice((g * pairs_per_group + q) * LANES, (g * pairs_per_group + q + 1) * LANES)
            o_ref[:, blk] = (out * scale * onw_ref[:, blk]).astype(o_ref.dtype)


def _ssd(xbc, z, dt, dtb_row, alog_row, dsk_row, onw_row, bsz, seq):
    rows = SSD_CHUNK
    steps = seq // rows
    row = lambda n: pl.BlockSpec((rows, n), lambda b, s: (b * steps + s, 0))
    head_of_channel = jnp.arange(C_D_INNER, dtype=jnp.int32) // C_HEAD_DIM
    expand = (jnp.arange(2 * LANES, dtype=jnp.int32)[:, None] % LANES == head_of_channel[None, :]).astype(BF16)
    return pl.pallas_call(
        _ssd_kernel,
        grid=(bsz, steps),
        in_specs=[row(C_XBC), row(C_D_INNER), row(LANES),
                  _const_spec((1, LANES)), _const_spec((1, LANES)), _const_spec((1, C_D_INNER)),
                  _const_spec((1, C_D_INNER)), _const_spec((2 * LANES, C_D_INNER))],
        out_specs=row(C_D_INNER),
        out_shape=jax.ShapeDtypeStruct((bsz * seq, C_D_INNER), BF16),
        scratch_shapes=[pltpu.VMEM((C_GROUPS, C_STATE, C_GROUP_WIDTH), F32)],
        compiler_params=_cparams("parallel", "arbitrary"),
        name="ssd",
    )(xbc, z, dt, dtb_row, alog_row, dsk_row, onw_row, expand)


def _pad_lanes(x, offset=0):
    x = x.reshape(-1, x.shape[-1])
    return jnp.pad(x, ((0, 0), (offset, LANES - offset - x.shape[-1])))


def _even_mixer(h, pre_w, post_w, w_in, conv_w, a_log, dt_bias, head_norm_w, w_out, bsz, seq):
    cuts = [A_WIDTH, 2 * A_WIDTH, 3 * A_WIDTH, 3 * A_WIDTH + 3 * B_WIDTH, 3 * A_WIDTH + 4 * B_WIDTH]
    w_in = w_in.astype(BF16)
    w_q = w_in[:, :cuts[0]] * A_HEAD_DIM ** -0.5
    w_k = w_in[:, cuts[0]:cuts[1]]
    w_v = w_in[:, cuts[1]:cuts[2]]
    w_qkv = w_in[:, cuts[2]:cuts[3]]
    w_z = w_in[:, cuts[3]:cuts[4]]
    w_sm = jnp.pad(w_in[:, cuts[4]:], ((0, 0), (0, LANES - 2 * B_HEADS)))
    no_bias = jnp.zeros((1, 3 * B_WIDTH), F32)
    qa, ka, va, qkv_b, z, sm = _norm_proj(h, pre_w, [w_q, w_k, w_v, w_qkv, w_z, w_sm],
                                          [F32, F32, F32, F32, F32, F32], seq, 3, conv_w, no_bias,
                                          slabbed=(0, 1, 2))
    o_a = _dilated_attention(qa, ka, va, bsz, seq)
    o_b = _gdn(qkv_b, z, sm, _pad_lanes(a_log, B_HEADS), _pad_lanes(dt_bias, B_HEADS),
               head_norm_w.reshape(1, -1), bsz, seq)
    w_o = w_out.astype(BF16)
    return post_w, [o_a, o_b], [w_o[:A_WIDTH], w_o[A_WIDTH:]]


def _odd_mixer(h, pre_w, post_w, w_in, conv_w, conv_b, dt_bias, a_log, d_skip, out_norm_w, w_out, bsz, seq):
    w_in = w_in.astype(BF16)
    w_z = w_in[:, :C_D_INNER]
    w_xbc = w_in[:, C_D_INNER:C_D_INNER + C_XBC]
    w_dt = jnp.pad(w_in[:, C_D_INNER + C_XBC:], ((0, 0), (0, LANES - C_HEADS)))
    z, xbc, dt = _norm_proj(h, pre_w, [w_z, w_xbc, w_dt], [F32, F32, F32], seq, 1, conv_w, conv_b.reshape(1, -1))
    y = _ssd(xbc, z, dt, _pad_lanes(dt_bias), _pad_lanes(a_log),
             jnp.repeat(d_skip, C_HEAD_DIM).reshape(1, -1), out_norm_w.reshape(1, -1), bsz, seq)
    return post_w, [y], [w_out.astype(BF16)]


def kernel(x, norm_w, ffn_w_gate, ffn_w_up, ffn_w_down, even_w_in, even_conv_w, even_a_log, even_dt_bias,
           even_head_norm_w, even_w_out, odd_w_in, odd_conv_w, odd_conv_b, odd_dt_bias, odd_a_log, odd_d_skip,
           odd_out_norm_w, odd_w_out):
    bsz, seq, d = x.shape
    depth = norm_w.shape[0]
    h = x.reshape(bsz * seq, d)
    wg, wu, wd = ffn_w_gate.astype(BF16), ffn_w_up.astype(BF16), ffn_w_down.astype(BF16)
    for layer in range(depth):
        nw = norm_w[layer][:, None, :]
        i = layer // 2
        h = _ffn(h, nw[0], nw[1], wg[layer, 0], wu[layer, 0], wd[layer, 0])
        if layer % 2 == 0:
            mixer = _even_mixer(h, nw[2], nw[3], even_w_in[i], even_conv_w[i], even_a_log[i], even_dt_bias[i],
                                even_head_norm_w[i], even_w_out[i], bsz, seq)
        else:
            mixer = _odd_mixer(h, nw[2], nw[3], odd_w_in[i], odd_conv_w[i], odd_conv_b[i], odd_dt_bias[i],
                               odd_a_log[i], odd_d_skip[i], odd_out_norm_w[i], odd_w_out[i], bsz, seq)
        h = _ffn(h, nw[4], nw[5], wg[layer, 1], wu[layer, 1], wd[layer, 1], mixer=mixer)
    return h.reshape(bsz, seq, d)
```
